```python
import math
import jax, jax.numpy as jnp
from jax import lax
import numpy as np


D_MODEL = 2048
BATCH = 4
SEQ = 8192
DEPTH = 1

GRID_W = 64
CTX_LEN = 256
N_HEADS_A = 8
HEAD_DIM_A = 64
V_DIM_A = 2 * HEAD_DIM_A
QK_WIDTH_A = N_HEADS_A * 2 * HEAD_DIM_A
WIDTH_A = N_HEADS_A * V_DIM_A
N_HEADS_B = 8
HEAD_DIM_B = 128
WIDTH_B = N_HEADS_B * HEAD_DIM_B
WIN_H = 8
WIN_W = 16
N_EXPERTS = 16
EXPERT_FF = 2048
CAPACITY_FACTOR = 2
Q_BLOCK = 128
ROPE_BASE = 10000.0
EPS = 1e-6
NEG_INF = -1e30

OFF_QA = 0
OFF_QB = OFF_QA + QK_WIDTH_A
OFF_GATE = OFF_QB + WIDTH_B
OFF_KA = OFF_GATE + 2 * D_MODEL
OFF_VA = OFF_KA + QK_WIDTH_A
OFF_KB = OFF_VA + WIDTH_A
OFF_VB = OFF_KB + WIDTH_B
PROJ_DIM = OFF_VB + WIDTH_B

kernel_name = 'hybrid_diffattn_natten_ecmoe_dit_block'


def rms_norm(x, g):
    xf = x.astype(jnp.float32)
    y = xf * lax.rsqrt(jnp.mean(xf * xf, axis=-1, keepdims=True) + EPS)
    return (y * g.astype(jnp.float32)).astype(x.dtype)


def modulate(x, shift, scale):
    return x * (1.0 + scale) + shift


def heads(t, n_heads):
    b, n, _ = t.shape
    return t.reshape(b, n, n_heads, -1).transpose(0, 2, 1, 3)


def merge_heads(t):
    b, h, n, dh = t.shape
    return t.transpose(0, 2, 1, 3).reshape(b, n, h * dh)


def axial_rope_tables(n_tokens):
    t = jnp.arange(n_tokens, dtype=jnp.int32)
    row = (t // GRID_W).astype(jnp.float32)
    col = (t % GRID_W).astype(jnp.float32)
    half = HEAD_DIM_A // 2
    inv_freq = ROPE_BASE ** (-jnp.arange(0, half, 2, dtype=jnp.float32) / half)

    def tab(pos):
        ang = pos[:, None] * inv_freq[None, :]
        ang = jnp.concatenate([ang, ang], axis=-1)
        return jnp.cos(ang), jnp.sin(ang)

    cr, sr = tab(row)
    cc, sc = tab(col)
    return jnp.concatenate([cr, cc], axis=-1), jnp.concatenate([sr, sc], axis=-1)


def rotate_half(x):
    x1, x2 = jnp.split(x, 2, axis=-1)
    return jnp.concatenate([-x2, x1], axis=-1)


def apply_axial_rope(x, cos, sin):
    xr, xc = jnp.split(x, 2, axis=-1)
    rot = jnp.concatenate([rotate_half(xr), rotate_half(xc)], axis=-1)
    return (x * cos + rot * sin).astype(x.dtype)


def split_kv(t):
    k_a = heads(t[..., 0:OFF_VA - OFF_KA], N_HEADS_A)
    v_a = heads(t[..., OFF_VA - OFF_KA:OFF_KB - OFF_KA], N_HEADS_A)
    k_b = heads(t[..., OFF_KB - OFF_KA:OFF_VB - OFF_KA], N_HEADS_B)
    v_b = heads(t[..., OFF_VB - OFF_KA:PROJ_DIM - OFF_KA], N_HEADS_B)
    return k_a, v_a, k_b, v_b


def diff_pair(t, gain):
    return rms_norm(t[..., :HEAD_DIM_A], gain), rms_norm(t[..., HEAD_DIM_A:], gain)


def diff_attend(q1, q2, k1, k2, v, lam):
    scale = HEAD_DIM_A ** -0.5
    p1 = jax.nn.softmax(jnp.einsum('bhqd,bhkd->bhqk', q1, k1).astype(jnp.float32) * scale, axis=-1)
    p2 = jax.nn.softmax(jnp.einsum('bhqd,bhkd->bhqk', q2, k2).astype(jnp.float32) * scale, axis=-1)
    att = (p1 - lam * p2).astype(v.dtype)
    return jnp.einsum('bhqk,bhkd->bhqd', att, v)


def diff_attention_blocks(q1, q2, k1_all, k2_all, v_all, lam):
    b, h, s, d = q1.shape
    nb = s // Q_BLOCK

    def blk(a):
        return a.reshape(b, h, nb, Q_BLOCK, d).transpose(2, 0, 1, 3, 4)

    out = lax.map(lambda qs: diff_attend(qs[0], qs[1], k1_all, k2_all, v_all, lam), (blk(q1), blk(q2)))
    return out.transpose(1, 2, 0, 3, 4).reshape(b, h, s, -1)


def dense_attend(q, k, v):
    scale = q.shape[-1] ** -0.5
    p = jax.nn.softmax(jnp.einsum('bhqd,bhkd->bhqk', q, k).astype(jnp.float32) * scale, axis=-1)
    return jnp.einsum('bhqk,bhkd->bhqd', p.astype(v.dtype), v)


def neighbourhood_attention(q, k, v, k_ctx, v_ctx, rpb, rows):
    b, h, s, dh = q.shape
    n_ctx = k_ctx.shape[2]
    kh = min(WIN_H, rows)
    k_grid = k.reshape(b, h, rows, GRID_W, dh)
    v_grid = v.reshape(b, h, rows, GRID_W, dh)
    q_rows = q.reshape(b, h, rows, GRID_W, dh).transpose(2, 0, 1, 3, 4)
    col = jnp.arange(GRID_W, dtype=jnp.int32)
    col_start = jnp.clip(col - WIN_W // 2, 0, GRID_W - WIN_W)
    in_win = (col[None, :] >= col_start[:, None]) & (col[None, :] < col_start[:, None] + WIN_W)
    col_mask = jnp.tile(in_win, (1, kh))
    idx_c = jnp.clip(col[None, :] - col[:, None] + WIN_W - 1, 0, 2 * WIN_W - 2)
    scale = dh ** -0.5

    def row_step(args):
        r, q_r = args
        r_start = jnp.clip(r - kh // 2, 0, rows - kh)
        k_blk = lax.dynamic_slice_in_dim(k_grid, r_start, kh, axis=2).reshape(b, h, kh * GRID_W, dh)
        v_blk = lax.dynamic_slice_in_dim(v_grid, r_start, kh, axis=2).reshape(b, h, kh * GRID_W, dh)
        idx_r = r_start + jnp.arange(kh, dtype=jnp.int32) - r + WIN_H - 1
        bias = rpb[:, idx_r[:, None, None], idx_c[None, :, :]]
        bias = bias.transpose(0, 2, 1, 3).reshape(h, GRID_W, kh * GRID_W).astype(jnp.float32)
        s_lat = jnp.einsum('bhqd,bhkd->bhqk', q_r, k_blk).astype(jnp.float32) * scale + bias[None]
        s_lat = jnp.where(col_mask, s_lat, NEG_INF)
        s_ctx = jnp.einsum('bhqd,bhkd->bhqk', q_r, k_ctx).astype(jnp.float32) * scale
        p = jax.nn.softmax(jnp.concatenate([s_ctx, s_lat], axis=-1), axis=-1).astype(v.dtype)
        return (jnp.einsum('bhqk,bhkd->bhqd', p[..., :n_ctx], v_ctx)
                + jnp.einsum('bhqk,bhkd->bhqd', p[..., n_ctx:], v_blk))

    out = lax.map(row_step, (jnp.arange(rows, dtype=jnp.int32), q_rows))
    return out.transpose(1, 2, 0, 3, 4).reshape(b, h, s, dh)


def gated_merge(gate_logits, y_a, y_b, w_a, w_b, w_o):
    g_a = jax.nn.sigmoid(gate_logits[..., :D_MODEL])
    g_b = jax.nn.sigmoid(gate_logits[..., D_MODEL:])
    return (g_a * (y_a @ w_a) + g_b * (y_b @ w_b)) @ w_o


def expert_choice_ffn(h, w_router, w_gate, w_up, w_down):
    b, n, d = h.shape
    cap = CAPACITY_FACTOR * n // N_EXPERTS
    aff = jax.nn.softmax((h @ w_router).astype(jnp.float32), axis=-1)
    g, idx = lax.top_k(aff.transpose(0, 2, 1), cap)
    xe = jax.vmap(lambda hb, ib: hb[ib])(h, idx)
    a = jnp.einsum('becd,edf->becf', xe, w_gate)
    u = jnp.einsum('becd,edf->becf', xe, w_up)
    y = jnp.einsum('becf,efd->becd', jax.nn.silu(a) * u, w_down)
    y = y * g[..., None].astype(y.dtype)

    def combine(yb, ib):
        return jnp.zeros((n, d), h.dtype).at[ib.reshape(-1)].add(yb.reshape(-1, d))

    return jax.vmap(combine)(y, idx)


def setup_inputs(seed: int = 0) -> dict:
    key = jax.random.key(seed)
    ks = jax.random.split(key, 32)
    d = D_MODEL

    def nrm(k, shape, scale):
        return jax.random.normal(k, shape, jnp.float32) * scale

    return {
        'x': nrm(ks[0], (BATCH, SEQ, d), 1.0),
        'c': nrm(ks[1], (BATCH, d), 1.0),
        'ctx': nrm(ks[2], (BATCH, CTX_LEN, d), 1.0),
        'c_ctx': nrm(ks[3], (d,), 1.0),
        'w_mod': nrm(ks[4], (DEPTH, d, 6 * d), 0.5 * d ** -0.5),
        'b_mod': nrm(ks[5], (DEPTH, 6 * d), 0.02),
        'g_norm1': 1.0 + nrm(ks[6], (DEPTH, d), 0.05),
        'g_norm2': 1.0 + nrm(ks[7], (DEPTH, d), 0.05),
        'w_in': nrm(ks[8], (DEPTH, d, PROJ_DIM), d ** -0.5),
        'q_gain_a': 1.0 + nrm(ks[9], (DEPTH, HEAD_DIM_A), 0.05),
        'k_gain_a': 1.0 + nrm(ks[10], (DEPTH, HEAD_DIM_A), 0.05),
        'lam_q1': nrm(ks[11], (DEPTH, HEAD_DIM_A), 0.1),
        'lam_k1': nrm(ks[12], (DEPTH, HEAD_DIM_A), 0.1),
        'lam_q2': nrm(ks[13], (DEPTH, HEAD_DIM_A), 0.1),
        'lam_k2': nrm(ks[14], (DEPTH, HEAD_DIM_A), 0.1),
        'subln_gain': 1.0 + nrm(ks[15], (DEPTH, V_DIM_A), 0.05),
        'q_gain_b': 1.0 + nrm(ks[16], (DEPTH, HEAD_DIM_B), 0.05),
        'k_gain_b': 1.0 + nrm(ks[17], (DEPTH, HEAD_DIM_B), 0.05),
        'rel_pos_bias': nrm(ks[18], (DEPTH, N_HEADS_B, 2 * WIN_H - 1, 2 * WIN_W - 1), 0.1),
        'w_branch_a': nrm(ks[19], (DEPTH, WIDTH_A, d), WIDTH_A ** -0.5),
        'w_branch_b': nrm(ks[20], (DEPTH, WIDTH_B, d), WIDTH_B ** -0.5),
        'w_out': nrm(ks[21], (DEPTH, d, d), d ** -0.5),
        'w_router': nrm(ks[22], (DEPTH, d, N_EXPERTS), d ** -0.5),
        'w_exp_gate': nrm(ks[23], (DEPTH, N_EXPERTS, d, EXPERT_FF), d ** -0.5),
        'w_exp_up': nrm(ks[24], (DEPTH, N_EXPERTS, d, EXPERT_FF), d ** -0.5),
        'w_exp_down': nrm(ks[25], (DEPTH, N_EXPERTS, EXPERT_FF, d), EXPERT_FF ** -0.5),
    }


def reference(x, c, ctx, c_ctx, w_mod, b_mod, g_norm1, g_norm2, w_in, q_gain_a, k_gain_a,
              lam_q1, lam_k1, lam_q2, lam_k2, subln_gain, q_gain_b, k_gain_b, rel_pos_bias,
              w_branch_a, w_branch_b, w_out, w_router, w_exp_gate, w_exp_up, w_exp_down):
    s = x.shape[1]
    rows = s // GRID_W
    cos, sin = axial_rope_tables(s)
    for l in range(DEPTH):
        update_ctx = l < DEPTH - 1
        lam_init = 0.8 - 0.6 * math.exp(-0.3 * l)
        lam = (jnp.exp(jnp.sum(lam_q1[l].astype(jnp.float32) * lam_k1[l].astype(jnp.float32)))
               - jnp.exp(jnp.sum(lam_q2[l].astype(jnp.float32) * lam_k2[l].astype(jnp.float32)))
               + lam_init)

        mod = jax.nn.silu(c) @ w_mod[l] + b_mod[l]
        mod_c = jax.nn.silu(c_ctx) @ w_mod[l] + b_mod[l]
        sh1, sc1, ga1, sh2, sc2, ga2 = jnp.split(mod[:, None, :], 6, axis=-1)
        csh1, csc1, cga1, csh2, csc2, cga2 = jnp.split(mod_c, 6, axis=-1)

        h = modulate(rms_norm(x, g_norm1[l]), sh1, sc1)
        hc = modulate(rms_norm(ctx, g_norm1[l]), csh1, csc1)
        proj = h @ w_in[l]
        if update_ctx:
            projc = hc @ w_in[l]
            projc_kv = projc[..., OFF_KA:]
        else:
            projc_kv = hc @ w_in[l][:, OFF_KA:]

        k_a_c, v_a_c, k_b_c, v_b_c = split_kv(projc_kv)
        k1c, k2c = diff_pair(k_a_c, k_gain_a[l])
        k_b_c = rms_norm(k_b_c, k_gain_b[l])

        k_a, v_a, k_b, v_b = split_kv(proj[..., OFF_KA:])
        q1, q2 = diff_pair(heads(proj[..., OFF_QA:OFF_QB], N_HEADS_A), q_gain_a[l])
        q1, q2 = apply_axial_rope(q1, cos, sin), apply_axial_rope(q2, cos, sin)
        k1, k2 = diff_pair(k_a, k_gain_a[l])
        k1, k2 = apply_axial_rope(k1, cos, sin), apply_axial_rope(k2, cos, sin)

        k1_all = jnp.concatenate([k1c, k1], axis=2)
        k2_all = jnp.concatenate([k2c, k2], axis=2)
        v_a_all = jnp.concatenate([v_a_c, v_a], axis=2)
        y_a = diff_attention_blocks(q1, q2, k1_all, k2_all, v_a_all, lam)
        y_a = merge_heads(rms_norm(y_a, subln_gain[l]) * (1.0 - lam_init))

        q_b = rms_norm(heads(proj[..., OFF_QB:OFF_GATE], N_HEADS_B), q_gain_b[l])
        k_b = rms_norm(k_b, k_gain_b[l])
        y_b = merge_heads(neighbourhood_attention(q_b, k_b, v_b, k_b_c, v_b_c, rel_pos_bias[l], rows))

        mix = gated_merge(proj[..., OFF_GATE:OFF_KA], y_a, y_b, w_branch_a[l], w_branch_b[l], w_out[l])
        x_new = x + ga1 * mix
        h2 = modulate(rms_norm(x_new, g_norm2[l]), sh2, sc2)
        x_new = x_new + ga2 * expert_choice_ffn(h2, w_router[l], w_exp_gate[l], w_exp_up[l], w_exp_down[l])

        if update_ctx:
            q1c, q2c = diff_pair(heads(projc[..., OFF_QA:OFF_QB], N_HEADS_A), q_gain_a[l])
            y_a_c = diff_attend(q1c, q2c, k1c, k2c, v_a_c, lam)
            y_a_c = merge_heads(rms_norm(y_a_c, subln_gain[l]) * (1.0 - lam_init))
            q_b_c = rms_norm(heads(projc[..., OFF_QB:OFF_GATE], N_HEADS_B), q_gain_b[l])
            y_b_c = merge_heads(dense_attend(q_b_c, k_b_c, v_b_c))
            mix_c = gated_merge(projc[..., OFF_GATE:OFF_KA], y_a_c, y_b_c, w_branch_a[l], w_branch_b[l], w_out[l])
            ctx_new = ctx + cga1 * mix_c
            h2c = modulate(rms_norm(ctx_new, g_norm2[l]), csh2, csc2)
            ctx = ctx_new + cga2 * expert_choice_ffn(h2c, w_router[l], w_exp_gate[l], w_exp_up[l], w_exp_down[l])
        x = x_new
    return x
```

```python
import functools
import math

import jax
import jax.numpy as jnp
from jax import lax
from jax.experimental import pallas as pl
from jax.experimental.pallas import tpu as pltpu

GRID_W = 64
N_HEADS_A = 8
HEAD_DIM_A = 64
N_HEADS_B = 8
HEAD_DIM_B = 128
WIN_H = 8
WIN_W = 16
CAPACITY_FACTOR = 2
ROPE_BASE = 10000.0
EPS = 1e-6
NEG_INF = -1e30

LANES = 128
BF16_SUBLANES = 16
VMEM_LIMIT_BYTES = 56 * 1024 * 1024

F32 = jnp.float32
BF16 = jnp.bfloat16


def _params(*sem):
    return pltpu.CompilerParams(dimension_semantics=sem, vmem_limit_bytes=VMEM_LIMIT_BYTES)


def _pick_tile(n, cap, mult):
    best = None
    for t in range(mult, min(n, cap) + 1, mult):
        if n % t == 0:
            best = t
    assert best is not None, (n, cap, mult)
    return best


def _mod_kernel(cc_ref, w_ref, b_ref, o_ref):
    cc = cc_ref[...]
    act = cc * jax.nn.sigmoid(cc)
    o_ref[...] = jnp.dot(act, w_ref[...], preferred_element_type=F32,
                         precision=lax.Precision.HIGHEST) + b_ref[...]


def adaln_mod(cc, w_mod, b_mod):
    r, d = cc.shape
    n = w_mod.shape[1]
    tn = _pick_tile(n, 1024, LANES)
    return pl.pallas_call(
        _mod_kernel,
        grid=(n // tn,),
        in_specs=[pl.BlockSpec((r, d), lambda j: (0, 0)),
                  pl.BlockSpec((d, tn), lambda j: (0, j)),
                  pl.BlockSpec((1, tn), lambda j: (0, j))],
        out_specs=pl.BlockSpec((r, tn), lambda j: (0, j)),
        out_shape=jax.ShapeDtypeStruct((r, n), F32),
        compiler_params=_params("parallel"),
        name="adaln_mod",
    )(cc, w_mod, b_mod.reshape(1, n))


def _norm_mod_kernel(ctx_ref, x_ref, g_ref, sh_ref, sc_ref, csh_ref, csc_ref, o_ref):
    def emit(src_ref, shift, scale):
        xf = src_ref[0]
        ms = jnp.mean(xf * xf, axis=-1, keepdims=True)
        y = xf * lax.rsqrt(ms + EPS) * g_ref[...]
        o_ref[0] = (y * (1.0 + scale) + shift).astype(o_ref.dtype)

    t = pl.program_id(1)

    @pl.when(t == 0)
    def _():
        emit(ctx_ref, csh_ref[...], csc_ref[...])

    @pl.when(t > 0)
    def _():
        emit(x_ref, sh_ref[0], sc_ref[0])


def norm_modulate_tokens(ctx, x, g, sh, sc, csh, csc):
    b, s, d = x.shape
    l = ctx.shape[1]
    assert s % l == 0
    nt = (l + s) // l
    vec = lambda a: a.reshape(1, d)
    return pl.pallas_call(
        _norm_mod_kernel,
        grid=(b, nt),
        in_specs=[pl.BlockSpec((1, l, d), lambda i, t: (i, 0, 0)),
                  pl.BlockSpec((1, l, d), lambda i, t: (i, jnp.maximum(t - 1, 0), 0)),
                  pl.BlockSpec((1, d), lambda i, t: (0, 0)),
                  pl.BlockSpec((1, 1, d), lambda i, t: (i, 0, 0)),
                  pl.BlockSpec((1, 1, d), lambda i, t: (i, 0, 0)),
                  pl.BlockSpec((1, d), lambda i, t: (0, 0)),
                  pl.BlockSpec((1, d), lambda i, t: (0, 0))],
        out_specs=pl.BlockSpec((1, l, d), lambda i, t: (i, t, 0)),
        out_shape=jax.ShapeDtypeStruct((b, l + s, d), BF16),
        compiler_params=_params("parallel", "arbitrary"),
        name="norm_modulate",
    )(ctx, x, vec(g), sh.reshape(b, 1, d), sc.reshape(b, 1, d), vec(csh), vec(csc))


def _matmul_kernel(x_ref, w_ref, o_ref):
    o_ref[...] = jnp.dot(x_ref[...], w_ref[...], preferred_element_type=F32).astype(o_ref.dtype)


def matmul_bf16(x, w):
    m, k = x.shape
    n = w.shape[1]
    tm = _pick_tile(m, 1024, 8 * BF16_SUBLANES)
    tn = _pick_tile(n, 1024, LANES)
    return pl.pallas_call(
        _matmul_kernel,
        grid=(m // tm, n // tn),
        in_specs=[pl.BlockSpec((tm, k), lambda i, j: (i, 0)),
                  pl.BlockSpec((k, tn), lambda i, j: (0, j))],
        out_specs=pl.BlockSpec((tm, tn), lambda i, j: (i, j)),
        out_shape=jax.ShapeDtypeStruct((m, n), BF16),
        compiler_params=_params("parallel", "arbitrary"),
        name="in_proj",
    )(x, w)


def _prep_kernel(*refs, n_heads, mode, rope, scale):
    if rope:
        x_ref, gain_ref, cos_ref, sa_ref, sb_ref, o_ref = refs
    elif mode != "copy":
        x_ref, gain_ref, o_ref = refs
    else:
        x_ref, o_ref = refs
    for h in range(n_heads):
        x = x_ref[0, :, h * LANES:(h + 1) * LANES]
        if mode == "copy":
            o_ref[0, h] = x
            continue
        x = x.astype(F32)
        x2 = x * x
        if mode == "pair":
            lo = lax.broadcasted_iota(jnp.int32, x.shape, 1) < HEAD_DIM_A
            s_lo = jnp.sum(jnp.where(lo, x2, 0.0), axis=-1, keepdims=True)
            s_hi = jnp.sum(jnp.where(lo, 0.0, x2), axis=-1, keepdims=True)
            r = jnp.where(lo, lax.rsqrt(s_lo * (1.0 / HEAD_DIM_A) + EPS),
                          lax.rsqrt(s_hi * (1.0 / HEAD_DIM_A) + EPS))
        else:
            r = lax.rsqrt(jnp.mean(x2, axis=-1, keepdims=True) + EPS)
        x = x * r * gain_ref[...]
        if rope:
            x = (x * cos_ref[...] + pltpu.roll(x, LANES - 16, 1) * sa_ref[...]
                 + pltpu.roll(x, 16, 1) * sb_ref[...])
        if scale != 1.0:
            x = x * scale
        o_ref[0, h] = x.astype(o_ref.dtype)


def prep_heads(proj3, col_off, n_heads, tok_off, n_tok, *, mode, gain=None, rope_tabs=None, scale=1.0):
    b, t_all, _ = proj3.shape
    width = n_heads * LANES
    assert col_off % width == 0
    tt = _pick_tile(math.gcd(n_tok, tok_off) if tok_off else n_tok, 512, 8 * BF16_SUBLANES)
    assert tok_off % tt == 0 and n_tok % tt == 0
    cb, tb = col_off // width, tok_off // tt
    in_specs = [pl.BlockSpec((1, tt, width), lambda ti, i: (i, ti + tb, cb))]
    args = [proj3]
    if mode != "copy":
        in_specs.append(pl.BlockSpec((1, LANES), lambda ti, i: (0, 0)))
        args.append(gain.reshape(1, LANES).astype(F32))
    if rope_tabs is not None:
        for tab in rope_tabs:
            in_specs.append(pl.BlockSpec((tt, LANES), lambda ti, i: (ti, 0)))
            args.append(tab)
    return pl.pallas_call(
        functools.partial(_prep_kernel, n_heads=n_heads, mode=mode, rope=rope_tabs is not None, scale=scale),
        grid=(n_tok // tt, b),
        in_specs=in_specs,
        out_specs=pl.BlockSpec((1, n_heads, tt, LANES), lambda ti, i: (i, 0, ti, 0)),
        out_shape=jax.ShapeDtypeStruct((b, n_heads, n_tok, LANES), BF16),
        compiler_params=_params("parallel", "arbitrary"),
        name="prep_" + mode,
    )(*args)


def rope_tables(n_lat, n_ctx):
    t = jnp.arange(n_lat, dtype=jnp.int32)
    row = (t // GRID_W).astype(F32)
    col = (t % GRID_W).astype(F32)
    half = HEAD_DIM_A // 2
    inv_freq = ROPE_BASE ** (-jnp.arange(0, half, 2, dtype=F32) / half)

    def tab(pos):
        ang = pos[:, None] * inv_freq[None, :]
        ang = jnp.concatenate([ang, ang], axis=-1)
        return jnp.cos(ang), jnp.sin(ang)

    cr, sr = tab(row)
    cc, sc = tab(col)
    cos = jnp.tile(jnp.concatenate([cr, cc], axis=-1), (1, LANES // HEAD_DIM_A))
    sin = jnp.tile(jnp.concatenate([sr, sc], axis=-1), (1, LANES // HEAD_DIM_A))
    first = (jnp.arange(LANES) % (half)) < (half // 2)
    sa = jnp.where(first[None, :], -sin, 0.0)
    sb = jnp.where(first[None, :], 0.0, sin)
    pad = lambda a, v: jnp.concatenate([jnp.full((n_ctx, LANES), v, F32), a], axis=0)
    return pad(cos, 1.0), pad(sa, 0.0), pad(sb, 0.0)


def _diff_attn_kernel(lamv_ref, q_ref, k_ref, v_ref, sg_ref, o_ref,
                      m1, l1, a1, m2, l2, a2, *, tk, lam_init):
    n_k = k_ref.shape[2] // tk
    q = q_ref[0, 0]
    lo = lax.broadcasted_iota(jnp.int32, q.shape, 1) < HEAD_DIM_A
    zero = jnp.zeros_like(q)
    qs = (jnp.where(lo, q, zero), jnp.where(lo, zero, q))
    stats = ((m1, l1, a1), (m2, l2, a2))
    for m_s, l_s, a_s in stats:
        m_s[...] = jnp.full(m_s.shape, -jnp.inf, F32)
        l_s[...] = jnp.zeros(l_s.shape, F32)
        a_s[...] = jnp.zeros(a_s.shape, F32)

    def body(kc, carry):
        off = pl.multiple_of(kc * tk, tk)
        k = k_ref[0, 0, pl.ds(off, tk), :]
        v = v_ref[0, 0, pl.ds(off, tk), :]
        for qq, (m_s, l_s, a_s) in zip(qs, stats):
            s = lax.dot_general(qq, k, (((1,), (1,)), ((), ())), preferred_element_type=F32)
            m_old = m_s[...]
            m_new = jnp.maximum(m_old, jnp.max(s, axis=-1, keepdims=True))
            alpha = jnp.exp(m_old - m_new)
            p = jnp.exp(s - m_new)
            l_s[...] = alpha * l_s[...] + jnp.sum(p, axis=-1, keepdims=True)
            a_s[...] = alpha * a_s[...] + jnp.dot(p.astype(BF16), v, preferred_element_type=F32)
            m_s[...] = m_new
        return carry

    lax.fori_loop(0, n_k, body, 0)

    lv = lamv_ref[...]
    lam = (jnp.exp(jnp.sum(lv[0:1] * lv[1:2], axis=-1, keepdims=True))
           - jnp.exp(jnp.sum(lv[2:3] * lv[3:4], axis=-1, keepdims=True)) + lam_init)
    y = a1[...] / l1[...] - lam * (a2[...] / l2[...])
    y = y * lax.rsqrt(jnp.mean(y * y, axis=-1, keepdims=True) + EPS) * sg_ref[...]
    o_ref[0] = (y * (1.0 - lam_init)).astype(o_ref.dtype)


def diff_attention(lamv, q, k, v, subln_gain, lam_init):
    b, h, s, _ = q.shape
    t = k.shape[2]
    tq = _pick_tile(s, 512, 8 * BF16_SUBLANES)
    tk = _pick_tile(t, 1024, LANES)
    stat = lambda w: pltpu.VMEM((tq, w), F32)
    return pl.pallas_call(
        functools.partial(_diff_attn_kernel, tk=tk, lam_init=lam_init),
        grid=(b, h, s // tq),
        in_specs=[pl.BlockSpec((4, HEAD_DIM_A), lambda i, j, qi: (0, 0)),
                  pl.BlockSpec((1, 1, tq, LANES), lambda i, j, qi: (i, j, qi, 0)),
                  pl.BlockSpec((1, 1, t, LANES), lambda i, j, qi: (i, j, 0, 0)),
                  pl.BlockSpec((1, 1, t, LANES), lambda i, j, qi: (i, j, 0, 0)),
                  pl.BlockSpec((1, LANES), lambda i, j, qi: (0, 0))],
        out_specs=pl.BlockSpec((1, tq, LANES), lambda i, j, qi: (i, qi, j)),
        out_shape=jax.ShapeDtypeStruct((b, s, h * LANES), BF16),
        scratch_shapes=[stat(1), stat(1), stat(LANES), stat(1), stat(1), stat(LANES)],
        compiler_params=_params("parallel", "parallel", "arbitrary"),
        name="diff_attention",
    )(lamv, q, k, v, subln_gain.reshape(1, LANES).astype(F32))


def _na_kernel(q_ref, k_ref, v_ref, bias_ref, o_ref, *, n_ctx, rows, kh):
    win = kh * GRID_W
    k_ctx = k_ref[0, 0, 0:n_ctx, :]
    v_ctx = v_ref[0, 0, 0:n_ctx, :]
    dn = (((1,), (1,)), ((), ()))

    def body(r, carry):
        r_start = jnp.clip(r - kh // 2, 0, rows - kh)
        q_r = q_ref[0, 0, pl.ds(pl.multiple_of(r * GRID_W, GRID_W), GRID_W), :]
        k_off = pl.multiple_of(n_ctx + r_start * GRID_W, GRID_W)
        k_w = k_ref[0, 0, pl.ds(k_off, win), :]
        v_w = v_ref[0, 0, pl.ds(k_off, win), :]
        s_lat = lax.dot_general(q_r, k_w, dn, preferred_element_type=F32) + bias_ref[0, r - r_start]
        s_ctx = lax.dot_general(q_r, k_ctx, dn, preferred_element_type=F32)
        m = jnp.maximum(jnp.max(s_lat, axis=-1, keepdims=True), jnp.max(s_ctx, axis=-1, keepdims=True))
        p_lat = jnp.exp(s_lat - m)
        p_ctx = jnp.exp(s_ctx - m)
        den = jnp.sum(p_lat, axis=-1, keepdims=True) + jnp.sum(p_ctx, axis=-1, keepdims=True)
        y = (jnp.dot(p_ctx.astype(BF16), v_ctx, preferred_element_type=F32)
             + jnp.dot(p_lat.astype(BF16), v_w, preferred_element_type=F32))
        o_ref[0, pl.ds(pl.multiple_of(r * GRID_W, GRID_W), GRID_W), :] = (y / den).astype(o_ref.dtype)
        return carry

    lax.fori_loop(0, rows, body, 0, unroll=2)


def na_bias_table(rpb, kh):
    col = jnp.arange(GRID_W, dtype=jnp.int32)
    col_start = jnp.clip(col - WIN_W // 2, 0, GRID_W - WIN_W)
    in_win = (col[None, :] >= col_start[:, None]) & (col[None, :] < col_start[:, None] + WIN_W)
    idx_c = jnp.clip(col[None, :] - col[:, None] + WIN_W - 1, 0, 2 * WIN_W - 2)
    v = jnp.arange(kh, dtype=jnp.int32)
    idx_r = jnp.arange(kh, dtype=jnp.int32)[None, :] - v[:, None] + WIN_H - 1
    bias = rpb[:, idx_r[:, :, None, None], idx_c[None, None, :, :]]
    bias = jnp.where(in_win[None, None, None], bias.astype(F32), NEG_INF)
    h = rpb.shape[0]
    return bias.transpose(0, 1, 3, 2, 4).reshape(h, kh, GRID_W, kh * GRID_W)


def neighbourhood_attention(q, k, v, bias, n_ctx):
    b, h, s, _ = q.shape
    t = k.shape[2]
    rows = s // GRID_W
    kh = min(WIN_H, rows)
    return pl.pallas_call(
        functools.partial(_na_kernel, n_ctx=n_ctx, rows=rows, kh=kh),
        grid=(b, h),
        in_specs=[pl.BlockSpec((1, 1, s, LANES), lambda i, j: (i, j, 0, 0)),
                  pl.BlockSpec((1, 1, t, LANES), lambda i, j: (i, j, 0, 0)),
                  pl.BlockSpec((1, 1, t, LANES), lambda i, j: (i, j, 0, 0)),
                  pl.BlockSpec((1, kh, GRID_W, kh * GRID_W), lambda i, j: (j, 0, 0, 0))],
        out_specs=pl.BlockSpec((1, s, LANES), lambda i, j: (i, 0, j)),
        out_shape=jax.ShapeDtypeStruct((b, s, h * LANES), BF16),
        compiler_params=_params("parallel", "parallel"),
        name="neighbourhood_attention",
    )(q, k, v, bias)


def _attention_branches(x, c, ctx, c_ctx, w_mod, b_mod, g_norm1, w_in, q_gain_a, k_gain_a, lamv, subln_gain,
                        q_gain_b, k_gain_b, rpb, lam_init):
    b, s, d = x.shape
    l = ctx.shape[1]
    t = l + s
    wa, wb = N_HEADS_A * LANES, N_HEADS_B * LANES
    off_qa, off_qb, off_gate = 0, wa, wa + wb
    off_ka = off_gate + 2 * d
    off_va, off_kb, off_vb = off_ka + wa, off_ka + 2 * wa, off_ka + 2 * wa + wb

    n_rows = -(-(b + 1) // 8) * 8
    cc = jnp.zeros((n_rows, d), F32).at[:b].set(c).at[b].set(c_ctx)
    mod = adaln_mod(cc, w_mod, b_mod)
    lat = [mod[:b, i * d:(i + 1) * d] for i in range(6)]
    cmod = [mod[b, i * d:(i + 1) * d] for i in range(6)]

    h = norm_modulate_tokens(ctx, x, g_norm1, lat[0], lat[1], cmod[0], cmod[1])
    proj = matmul_bf16(h.reshape(b * t, d), w_in.astype(BF16)).reshape(b, t, w_in.shape[1])

    tabs = rope_tables(s, l)
    tabs_q = tuple(a[l:] for a in tabs)
    two = lambda g: jnp.tile(g, LANES // HEAD_DIM_A)
    qa = prep_heads(proj, off_qa, N_HEADS_A, l, s, mode="pair", gain=two(q_gain_a), rope_tabs=tabs_q,
                    scale=HEAD_DIM_A ** -0.5)
    ka = prep_heads(proj, off_ka, N_HEADS_A, 0, t, mode="pair", gain=two(k_gain_a), rope_tabs=tabs)
    va = prep_heads(proj, off_va, N_HEADS_A, 0, t, mode="copy")
    qb = prep_heads(proj, off_qb, N_HEADS_B, l, s, mode="full", gain=q_gain_b, scale=HEAD_DIM_B ** -0.5)
    kb = prep_heads(proj, off_kb, N_HEADS_B, 0, t, mode="full", gain=k_gain_b)
    vb = prep_heads(proj, off_vb, N_HEADS_B, 0, t, mode="copy")

    y_a = diff_attention(lamv, qa, ka, va, subln_gain, lam_init)
    kh = min(WIN_H, s // GRID_W)
    y_b = neighbourhood_attention(qb, kb, vb, na_bias_table(rpb, kh), l)
    return lat, proj, y_a, y_b, off_gate


def _merge_kernel(ya_ref, yb_ref, ga_ref, gb_ref, x_ref, wa_ref, wb_ref, wo_ref, g1_ref, gn_ref, sh_ref, sc_ref,
                  wrh_ref, wrl_ref, xo_ref, h2_ref, aff_ref, *, n_exp):
    ta = jnp.dot(ya_ref[0], wa_ref[...], preferred_element_type=F32)
    tb = jnp.dot(yb_ref[0], wb_ref[...], preferred_element_type=F32)
    u = jax.nn.sigmoid(ga_ref[0].astype(F32)) * ta + jax.nn.sigmoid(gb_ref[0].astype(F32)) * tb
    mix = jnp.dot(u.astype(BF16), wo_ref[...], preferred_element_type=F32)
    xn = x_ref[0] + g1_ref[0] * mix
    xo_ref[0] = xn
    ms = jnp.mean(xn * xn, axis=-1, keepdims=True)
    h2 = (xn * lax.rsqrt(ms + EPS) * gn_ref[...]) * (1.0 + sc_ref[0]) + sh_ref[0]
    hi = h2.astype(BF16)
    h2_ref[0] = hi
    lo = (h2 - hi.astype(F32)).astype(BF16)
    logits = (jnp.dot(hi, wrh_ref[...], preferred_element_type=F32)
              + jnp.dot(lo, wrh_ref[...], preferred_element_type=F32)
              + jnp.dot(hi, wrl_ref[...], preferred_element_type=F32))
    valid = lax.broadcasted_iota(jnp.int32, logits.shape, 1) < n_exp
    logits = jnp.where(valid, logits, -jnp.inf)
    p = jnp.exp(logits - jnp.max(logits, axis=-1, keepdims=True))
    aff_ref[0] = p / jnp.sum(p, axis=-1, keepdims=True)


def merge_and_route(y_a, y_b, proj3, off_gate, n_ctx, x, w_a, w_b, w_o, ga1, g_norm2, sh2, sc2, w_router):
    b, s, d = x.shape
    n_exp = w_router.shape[1]
    tm = _pick_tile(math.gcd(s, n_ctx), 256, 8 * BF16_SUBLANES)
    assert off_gate % d == 0 and n_exp <= LANES
    tb, gb = n_ctx // tm, off_gate // d
    wr = jnp.zeros((d, LANES), F32).at[:, :n_exp].set(w_router)
    wr_hi = wr.astype(BF16)
    wr_lo = (wr - wr_hi.astype(F32)).astype(BF16)
    const = lambda shape: pl.BlockSpec(shape, lambda i, t: (0,) * len(shape), pipeline_mode=pl.Buffered(1))
    tok = lambda w: pl.BlockSpec((1, tm, w), lambda i, t: (i, t, 0))
    per_b = pl.BlockSpec((1, 1, d), lambda i, t: (i, 0, 0))
    return pl.pallas_call(
        functools.partial(_merge_kernel, n_exp=n_exp),
        grid=(b, s // tm),
        in_specs=[tok(y_a.shape[2]), tok(y_b.shape[2]),
                  pl.BlockSpec((1, tm, d), lambda i, t: (i, t + tb, gb)),
                  pl.BlockSpec((1, tm, d), lambda i, t: (i, t + tb, gb + 1)),
                  tok(d), const(w_a.shape), const(w_b.shape), const(w_o.shape),
                  per_b, const((1, d)), per_b, per_b, const((d, LANES)), const((d, LANES))],
        out_specs=[tok(d), tok(d), tok(LANES)],
        out_shape=[jax.ShapeDtypeStruct((b, s, d), F32), jax.ShapeDtypeStruct((b, s, d), BF16),
                   jax.ShapeDtypeStruct((b, s, LANES), F32)],
        compiler_params=_params("parallel", "arbitrary"),
        name="merge_route",
    )(y_a, y_b, proj3, proj3, x, w_a.astype(BF16), w_b.astype(BF16), w_o.astype(BF16),
      ga1.reshape(b, 1, d), g_norm2.reshape(1, d), sh2.reshape(b, 1, d), sc2.reshape(b, 1, d), wr_hi, wr_lo)


def _prefix_incl(mask, tri_ref, ones_ref, blk_ref):
    e, nc, _ = mask.shape
    m2 = mask.reshape(e * nc, LANES).astype(BF16)
    within = jnp.dot(m2, tri_ref[...], preferred_element_type=F32)
    row_tot = jnp.dot(m2, ones_ref[...], preferred_element_type=F32)
    before = jnp.dot(blk_ref[...], row_tot.astype(BF16), preferred_element_type=F32)
    return (within + before).reshape(e, nc, LANES), before.reshape(e, nc, LANES)


def _topk_kernel(aff_ref, tri_ref, ones_ref, blk_ref, pos_ref, off_ref, *, cap):
    a = aff_ref[0]
    bits = pltpu.bitcast(a, jnp.int32)

    def count(mask):
        c = jnp.sum(jnp.where(mask, 1.0, 0.0), axis=2, keepdims=True)
        return jnp.sum(c, axis=1, keepdims=True)

    def step(i, cur):
        cand = cur | jnp.left_shift(jnp.int32(1), 30 - i)
        return jnp.where(count(bits >= cand) >= cap, cand, cur)

    thr = lax.fori_loop(0, 31, step, jnp.zeros((a.shape[0], 1, 1), jnp.int32))
    gt = bits > thr
    eq = bits == thr
    need = cap - count(gt)
    eq_f = jnp.where(eq, 1.0, 0.0)
    eq_incl, _ = _prefix_incl(eq_f, tri_ref, ones_ref, blk_ref)
    sel = gt | (eq & ((eq_incl - eq_f) < need))
    sel_f = jnp.where(sel, 1.0, 0.0)
    sel_incl, before = _prefix_incl(sel_f, tri_ref, ones_ref, blk_ref)
    pos_ref[0] = jnp.where(sel, sel_incl - sel_f, -1.0).astype(jnp.int32)
    off_ref[0] = before.astype(jnp.int32)


def expert_choice_select(aff_e, cap):
    b, e, s = aff_e.shape
    nc = s // LANES
    idx = jnp.arange(LANES)
    tri = (idx[:, None] <= idx[None, :]).astype(BF16)
    ones = jnp.ones((LANES, LANES), BF16)
    r = jnp.arange(e * nc)
    blk = ((r[:, None] // nc == r[None, :] // nc) & (r[None, :] % nc < r[:, None] % nc)).astype(BF16)
    blk4 = pl.BlockSpec((1, e, nc, LANES), lambda i: (i, 0, 0, 0))
    const = lambda n: pl.BlockSpec((n, n), lambda i: (0, 0))
    pos, off = pl.pallas_call(
        functools.partial(_topk_kernel, cap=cap),
        grid=(b,),
        in_specs=[blk4, const(LANES), const(LANES), const(e * nc)],
        out_specs=[blk4, blk4],
        out_shape=[jax.ShapeDtypeStruct((b, e, nc, LANES), jnp.int32)] * 2,
        compiler_params=_params("parallel"),
        name="expert_choice_select",
    )(aff_e.reshape(b, e, nc, LANES), tri, ones, blk)
    return pos.reshape(b, e, 1, s), off[..., 0]


def _gather_kernel(lo_ref, h_ref, pos_ref, o_ref, *, eg, win, nt, n_exp):
    i, g, t = pl.program_id(0), pl.program_id(1), pl.program_id(2)

    @pl.when(t == 0)
    def _():
        o_ref[...] = jnp.zeros(o_ref.shape, o_ref.dtype)

    h = h_ref[0]
    slot = lax.broadcasted_iota(jnp.int32, (win, h.shape[0]), 0)
    for k in range(eg):
        lo = pl.multiple_of(lo_ref[(i * n_exp + g * eg + k) * nt + t], BF16_SUBLANES)
        onehot = jnp.where(slot == pos_ref[0, k] - lo, 1.0, 0.0).astype(BF16)
        rows = jnp.dot(onehot, h, preferred_element_type=F32)
        cur = o_ref[0, k, pl.ds(lo, win), :]
        o_ref[0, k, pl.ds(lo, win), :] = (cur.astype(F32) + rows).astype(o_ref.dtype)


def expert_gather(h2, pos, lo_flat, cap, tm, win, eg=2):
    b, s, d = h2.shape
    n_exp = pos.shape[1]
    nt = s // tm
    cp = cap + win
    return pl.pallas_call(
        functools.partial(_gather_kernel, eg=eg, win=win, nt=nt, n_exp=n_exp),
        grid_spec=pltpu.PrefetchScalarGridSpec(
            num_scalar_prefetch=1,
            grid=(b, n_exp // eg, nt),
            in_specs=[pl.BlockSpec((1, tm, d), lambda i, g, t, lo: (i, t, 0)),
                      pl.BlockSpec((1, eg, 1, tm), lambda i, g, t, lo: (i, g, 0, t))],
            out_specs=pl.BlockSpec((1, eg, cp, d), lambda i, g, t, lo: (i, g, 0, 0))),
        out_shape=jax.ShapeDtypeStruct((b, n_exp, cp, d), BF16),
        compiler_params=_params("parallel", "parallel", "arbitrary"),
        name="expert_gather",
    )(lo_flat, h2, pos)


def _ffn_kernel(x_ref, wg_ref, wu_ref, wd_ref, o_ref, acc_ref, *, cap):
    f = pl.program_id(2)
    x = x_ref[0, 0]
    a = jnp.dot(x, wg_ref[0], preferred_element_type=F32)
    u = jnp.dot(x, wu_ref[0], preferred_element_type=F32)
    part = jnp.dot((a * jax.nn.sigmoid(a) * u).astype(BF16), wd_ref[0], preferred_element_type=F32)

    @pl.when(f == 0)
    def _():
        acc_ref[...] = part

    @pl.when(f > 0)
    def _():
        acc_ref[...] += part

    @pl.when(f == pl.num_programs(2) - 1)
    def _():
        o_ref[0, 0, 0:cap, :] = acc_ref[...].astype(o_ref.dtype)
        o_ref[0, 0, cap:, :] = jnp.zeros((o_ref.shape[2] - cap, o_ref.shape[3]), o_ref.dtype)


def expert_ffn(xe, w_gate, w_up, w_down, cap):
    b, n_exp, cp, d = xe.shape
    ff = w_gate.shape[2]
    tf = _pick_tile(ff, 512, LANES)
    return pl.pallas_call(
        functools.partial(_ffn_kernel, cap=cap),
        grid=(n_exp, b, ff // tf),
        in_specs=[pl.BlockSpec((1, 1, cap, d), lambda e, i, f: (i, e, 0, 0)),
                  pl.BlockSpec((1, d, tf), lambda e, i, f: (e, 0, f)),
                  pl.BlockSpec((1, d, tf), lambda e, i, f: (e, 0, f)),
                  pl.BlockSpec((1, tf, d), lambda e, i, f: (e, f, 0))],
        out_specs=pl.BlockSpec((1, 1, cp, d), lambda e, i, f: (i, e, 0, 0)),
        out_shape=jax.ShapeDtypeStruct((b, n_exp, cp, d), BF16),
        scratch_shapes=[pltpu.VMEM((cap, d), F32)],
        compiler_params=_params("parallel", "parallel", "arbitrary"),
        name="expert_ffn",
    )(xe, w_gate, w_up, w_down)


def _combine_kernel(lo_ref, y_hbm, pos_ref, g_ref, x_ref, ga_ref, o_ref, ybuf, sem, *, win, nt, n_exp):
    i, t = pl.program_id(0), pl.program_id(1)

    def window(e):
        lo = pl.multiple_of(lo_ref[(i * n_exp + e) * nt + t], BF16_SUBLANES)
        return lo, pltpu.make_async_copy(y_hbm.at[i, e, pl.ds(lo, win), :], ybuf.at[e % 2], sem.at[e % 2])

    window(0)[1].start()
    tm = x_ref.shape[1]
    slot = lax.broadcasted_iota(jnp.int32, (win, tm), 0)
    acc = jnp.zeros((tm, x_ref.shape[2]), F32)
    for e in range(n_exp):
        if e + 1 < n_exp:
            window(e + 1)[1].start()
        lo, copy = window(e)
        copy.wait()
        gated = jnp.where(slot == pos_ref[0, e] - lo, g_ref[0, e], 0.0).astype(BF16)
        acc = acc + lax.dot_general(gated, ybuf[e % 2], (((0,), (0,)), ((), ())), preferred_element_type=F32)
    o_ref[0] = x_ref[0] + ga_ref[0] * acc


def expert_combine(y, pos, gates, lo_flat, x_new, ga2, tm, win):
    b, s, d = x_new.shape
    n_exp = pos.shape[1]
    nt = s // tm
    sel = pl.BlockSpec((1, n_exp, 1, tm), lambda i, t, lo: (i, 0, 0, t))
    return pl.pallas_call(
        functools.partial(_combine_kernel, win=win, nt=nt, n_exp=n_exp),
        grid_spec=pltpu.PrefetchScalarGridSpec(
            num_scalar_prefetch=1,
            grid=(b, nt),
            in_specs=[pl.BlockSpec(memory_space=pl.ANY), sel, sel,
                      pl.BlockSpec((1, tm, d), lambda i, t, lo: (i, t, 0)),
                      pl.BlockSpec((1, 1, d), lambda i, t, lo: (i, 0, 0))],
            out_specs=pl.BlockSpec((1, tm, d), lambda i, t, lo: (i, t, 0)),
            scratch_shapes=[pltpu.VMEM((2, win, d), BF16), pltpu.SemaphoreType.DMA((2,))]),
        out_shape=jax.ShapeDtypeStruct((b, s, d), F32),
        compiler_params=_params("parallel", "arbitrary"),
        name="expert_combine",
    )(lo_flat, y, pos, gates, x_new, ga2.reshape(b, 1, d))


def expert_choice_moe(x_new, h2, aff, ga2, w_gate, w_up, w_down):
    b, s, d = x_new.shape
    n_exp = w_gate.shape[0]
    cap = CAPACITY_FACTOR * s // n_exp
    tm = 2 * LANES
    win = tm + BF16_SUBLANES
    aff_e = aff[:, :, :n_exp].transpose(0, 2, 1)
    pos, off = expert_choice_select(aff_e, cap)
    lo = off[:, :, ::tm // LANES]
    lo_flat = ((lo // BF16_SUBLANES) * BF16_SUBLANES).reshape(-1)
    xe = expert_gather(h2, pos, lo_flat, cap, tm, win)
    y = expert_ffn(xe, w_gate.astype(BF16), w_up.astype(BF16), w_down.astype(BF16), cap)
    return expert_combine(y, pos, aff_e.reshape(b, n_exp, 1, s), lo_flat, x_new, ga2, tm, win)


def kernel(x, c, ctx, c_ctx, w_mod, b_mod, g_norm1, g_norm2, w_in, q_gain_a, k_gain_a, lam_q1, lam_k1, lam_q2,
           lam_k2, subln_gain, q_gain_b, k_gain_b, rel_pos_bias, w_branch_a, w_branch_b, w_out, w_router,
           w_exp_gate, w_exp_up, w_exp_down):
    assert w_mod.shape[0] == 1, "single-layer block"
    lam_init = 0.8 - 0.6 * math.exp(-0.3 * 0)
    lamv = jnp.stack([lam_q1[0], lam_k1[0], lam_q2[0], lam_k2[0]]).astype(F32)
    lat, proj, y_a, y_b, off_gate = _attention_branches(
        x, c, ctx, c_ctx, w_mod[0], b_mod[0], g_norm1[0], w_in[0], q_gain_a[0], k_gain_a[0], lamv, subln_gain[0],
        q_gain_b[0], k_gain_b[0], rel_pos_bias[0], lam_init)
    _, _, ga1, sh2, sc2, ga2 = lat
    x_new, h2, aff = merge_and_route(y_a, y_b, proj, off_gate, ctx.shape[1], x, w_branch_a[0], w_branch_b[0],
                                     w_out[0], ga1, g_norm2[0], sh2, sc2, w_router[0])
    return expert_choice_moe(x_new, h2, aff, ga2, w_exp_gate[0], w_exp_up[0], w_exp_down[0])
```

```python
import functools
import math

import jax
import jax.numpy as jnp
from jax import lax
from jax.experimental import pallas as pl
from jax.experimental.pallas import tpu as pltpu

GRID_W = 64
N_HEADS_A = 8
HEAD_DIM_A = 64
N_HEADS_B = 8
HEAD_DIM_B = 128
WIN_H = 8
WIN_W = 16
CAPACITY_FACTOR = 2
ROPE_BASE = 10000.0
EPS = 1e-6
NEG_INF = -1e30

LANES = 128
BF16_SUBLANES = 16
VMEM_LIMIT_BYTES = 56 * 1024 * 1024

F32 = jnp.float32
BF16 = jnp.bfloat16


def _params(*sem):
    return pltpu.CompilerParams(dimension_semantics=sem, vmem_limit_bytes=VMEM_LIMIT_BYTES)


def _pick_tile(n, cap, mult):
    best = None
    for t in range(mult, min(n, cap) + 1, mult):
        if n % t == 0:
            best = t
    assert best is not None, (n, cap, mult)
    return best


def _mod_kernel(cc_ref, w_ref, b_ref, o_ref):
    cc = cc_ref[...]
    act = cc * jax.nn.sigmoid(cc)
    o_ref[...] = jnp.dot(act, w_ref[...], preferred_element_type=F32,
                         precision=lax.Precision.HIGHEST) + b_ref[...]


def adaln_mod(cc, w_mod, b_mod):
    r, d = cc.shape
    n = w_mod.shape[1]
    tn = _pick_tile(n, 1024, LANES)
    return pl.pallas_call(
        _mod_kernel,
        grid=(n // tn,),
        in_specs=[pl.BlockSpec((r, d), lambda j: (0, 0)),
                  pl.BlockSpec((d, tn), lambda j: (0, j)),
                  pl.BlockSpec((1, tn), lambda j: (0, j))],
        out_specs=pl.BlockSpec((r, tn), lambda j: (0, j)),
        out_shape=jax.ShapeDtypeStruct((r, n), F32),
        compiler_params=_params("parallel"),
        name="adaln_mod",
    )(cc, w_mod, b_mod.reshape(1, n))


def _norm_mod_kernel(ctx_ref, x_ref, g_ref, sh_ref, sc_ref, csh_ref, csc_ref, o_ref):
    def emit(src_ref, shift, scale):
        xf = src_ref[0]
        ms = jnp.mean(xf * xf, axis=-1, keepdims=True)
        y = xf * lax.rsqrt(ms + EPS) * g_ref[...]
        o_ref[0] = (y * (1.0 + scale) + shift).astype(o_ref.dtype)

    t = pl.program_id(1)

    @pl.when(t == 0)
    def _():
        emit(ctx_ref, csh_ref[...], csc_ref[...])

    @pl.when(t > 0)
    def _():
        emit(x_ref, sh_ref[0], sc_ref[0])


def norm_modulate_tokens(ctx, x, g, sh, sc, csh, csc):
    b, s, d = x.shape
    l = ctx.shape[1]
    assert s % l == 0
    nt = (l + s) // l
    vec = lambda a: a.reshape(1, d)
    return pl.pallas_call(
        _norm_mod_kernel,
        grid=(b, nt),
        in_specs=[pl.BlockSpec((1, l, d), lambda i, t: (i, 0, 0)),
                  pl.BlockSpec((1, l, d), lambda i, t: (i, jnp.maximum(t - 1, 0), 0)),
                  pl.BlockSpec((1, d), lambda i, t: (0, 0)),
                  pl.BlockSpec((1, 1, d), lambda i, t: (i, 0, 0)),
                  pl.BlockSpec((1, 1, d), lambda i, t: (i, 0, 0)),
                  pl.BlockSpec((1, d), lambda i, t: (0, 0)),
                  pl.BlockSpec((1, d), lambda i, t: (0, 0))],
        out_specs=pl.BlockSpec((1, l, d), lambda i, t: (i, t, 0)),
        out_shape=jax.ShapeDtypeStruct((b, l + s, d), BF16),
        compiler_params=_params("parallel", "arbitrary"),
        name="norm_modulate",
    )(ctx, x, vec(g), sh.reshape(b, 1, d), sc.reshape(b, 1, d), vec(csh), vec(csc))


def _matmul_kernel(x_ref, w_ref, o_ref):
    o_ref[...] = jnp.dot(x_ref[...], w_ref[...], preferred_element_type=F32).astype(o_ref.dtype)


def matmul_bf16(x, w):
    m, k = x.shape
    n = w.shape[1]
    tm = _pick_tile(m, 1024, 8 * BF16_SUBLANES)
    tn = _pick_tile(n, 1024, LANES)
    return pl.pallas_call(
        _matmul_kernel,
        grid=(m // tm, n // tn),
        in_specs=[pl.BlockSpec((tm, k), lambda i, j: (i, 0)),
                  pl.BlockSpec((k, tn), lambda i, j: (0, j))],
        out_specs=pl.BlockSpec((tm, tn), lambda i, j: (i, j)),
        out_shape=jax.ShapeDtypeStruct((m, n), BF16),
        compiler_params=_params("parallel", "arbitrary"),
        name="in_proj",
    )(x, w)


def _prep_kernel(*refs, n_heads, mode, rope, scale):
    if rope:
        x_ref, gain_ref, cos_ref, sa_ref, sb_ref, o_ref = refs
    elif mode != "copy":
        x_ref, gain_ref, o_ref = refs
    else:
        x_ref, o_ref = refs
    for h in range(n_heads):
        x = x_ref[0, :, h * LANES:(h + 1) * LANES]
        if mode == "copy":
            o_ref[0, h, :, 0:LANES] = x
            if o_ref.shape[3] > LANES:
                o_ref[0, h, :, LANES:] = jnp.ones((x.shape[0], o_ref.shape[3] - LANES), o_ref.dtype)
            continue
        x = x.astype(F32)
        x2 = x * x
        if mode == "pair":
            lo = lax.broadcasted_iota(jnp.int32, x.shape, 1) < HEAD_DIM_A
            s_lo = jnp.sum(jnp.where(lo, x2, 0.0), axis=-1, keepdims=True)
            s_hi = jnp.sum(jnp.where(lo, 0.0, x2), axis=-1, keepdims=True)
            r = jnp.where(lo, lax.rsqrt(s_lo * (1.0 / HEAD_DIM_A) + EPS),
                          lax.rsqrt(s_hi * (1.0 / HEAD_DIM_A) + EPS))
        else:
            r = lax.rsqrt(jnp.mean(x2, axis=-1, keepdims=True) + EPS)
        x = x * r * gain_ref[...]
        if rope:
            x = (x * cos_ref[...] + pltpu.roll(x, LANES - 16, 1) * sa_ref[...]
                 + pltpu.roll(x, 16, 1) * sb_ref[...])
        if scale != 1.0:
            x = x * scale
        o_ref[0, h] = x.astype(o_ref.dtype)


def prep_heads(proj3, col_off, n_heads, tok_off, n_tok, *, mode, gain=None, rope_tabs=None, scale=1.0,
               out_width=LANES):
    b, t_all, _ = proj3.shape
    width = n_heads * LANES
    assert col_off % width == 0
    tt = _pick_tile(math.gcd(n_tok, tok_off) if tok_off else n_tok, 512, 8 * BF16_SUBLANES)
    assert tok_off % tt == 0 and n_tok % tt == 0
    cb, tb = col_off // width, tok_off // tt
    in_specs = [pl.BlockSpec((1, tt, width), lambda ti, i: (i, ti + tb, cb))]
    args = [proj3]
    if mode != "copy":
        in_specs.append(pl.BlockSpec((1, LANES), lambda ti, i: (0, 0)))
        args.append(gain.reshape(1, LANES).astype(F32))
    if rope_tabs is not None:
        for tab in rope_tabs:
            in_specs.append(pl.BlockSpec((tt, LANES), lambda ti, i: (ti, 0)))
            args.append(tab)
    return pl.pallas_call(
        functools.partial(_prep_kernel, n_heads=n_heads, mode=mode, rope=rope_tabs is not None, scale=scale),
        grid=(n_tok // tt, b),
        in_specs=in_specs,
        out_specs=pl.BlockSpec((1, n_heads, tt, out_width), lambda ti, i: (i, 0, ti, 0)),
        out_shape=jax.ShapeDtypeStruct((b, n_heads, n_tok, out_width), BF16),
        compiler_params=_params("parallel", "arbitrary"),
        name="prep_" + mode,
    )(*args)


def rope_tables(n_lat, n_ctx):
    t = jnp.arange(n_lat, dtype=jnp.int32)
    row = (t // GRID_W).astype(F32)
    col = (t % GRID_W).astype(F32)
    half = HEAD_DIM_A // 2
    inv_freq = ROPE_BASE ** (-jnp.arange(0, half, 2, dtype=F32) / half)

    def tab(pos):
        ang = pos[:, None] * inv_freq[None, :]
        ang = jnp.concatenate([ang, ang], axis=-1)
        return jnp.cos(ang), jnp.sin(ang)

    cr, sr = tab(row)
    cc, sc = tab(col)
    cos = jnp.tile(jnp.concatenate([cr, cc], axis=-1), (1, LANES // HEAD_DIM_A))
    sin = jnp.tile(jnp.concatenate([sr, sc], axis=-1), (1, LANES // HEAD_DIM_A))
    first = (jnp.arange(LANES) % (half)) < (half // 2)
    sa = jnp.where(first[None, :], -sin, 0.0)
    sb = jnp.where(first[None, :], 0.0, sin)
    pad = lambda a, v: jnp.concatenate([jnp.full((n_ctx, LANES), v, F32), a], axis=0)
    return pad(cos, 1.0), pad(sa, 0.0), pad(sb, 0.0)


def _diff_attn_kernel(lamv_ref, q_ref, k_ref, v_ref, sg_ref, o_ref, m_s, a_s, qq_s, sa, sb, *, tk, lam_init):
    n_k = k_ref.shape[2] // tk
    tq = q_ref.shape[2]
    q = q_ref[0, 0]
    lo = lax.broadcasted_iota(jnp.int32, q.shape, 1) < HEAD_DIM_A
    zero = jnp.zeros_like(q)
    qq_s[0:tq, :] = jnp.where(lo, q, zero)
    qq_s[tq:, :] = jnp.where(lo, zero, q)
    m_s[...] = jnp.full(m_s.shape, -jnp.inf, F32)
    a_s[...] = jnp.zeros(a_s.shape, F32)

    def produce(c, s_ref):
        k = k_ref[0, 0, pl.ds(pl.multiple_of(c * tk, tk), tk), :]
        s_ref[...] = lax.dot_general(qq_s[...], k, (((1,), (1,)), ((), ())), preferred_element_type=F32)

    def consume(c, s_ref):
        v = v_ref[0, 0, pl.ds(pl.multiple_of(c * tk, tk), tk), :]
        s = s_ref[...]
        m_old = m_s[...]
        m_new = jnp.maximum(m_old, jnp.max(s, axis=-1, keepdims=True))
        alpha = jnp.exp2(m_old - m_new)
        p = jnp.exp2(s - jnp.tile(m_new, (1, tk // LANES)))
        a_s[...] = jnp.tile(alpha, (1, 2)) * a_s[...] + jnp.dot(p.astype(BF16), v, preferred_element_type=F32)
        m_s[...] = m_new

    bufs = (sa, sb)
    produce(0, sa)
    for c in range(n_k):
        if c + 1 < n_k:
            produce(c + 1, bufs[(c + 1) % 2])
        consume(c, bufs[c % 2])

    lv = lamv_ref[...]
    lam = (jnp.exp(jnp.sum(lv[0:1] * lv[1:2], axis=-1, keepdims=True))
           - jnp.exp(jnp.sum(lv[2:3] * lv[3:4], axis=-1, keepdims=True)) + lam_init)
    y = (a_s[0:tq, 0:LANES] / a_s[0:tq, LANES:] - lam * (a_s[tq:, 0:LANES] / a_s[tq:, LANES:]))
    y = y * lax.rsqrt(jnp.mean(y * y, axis=-1, keepdims=True) + EPS) * sg_ref[...]
    o_ref[0] = (y * (1.0 - lam_init)).astype(o_ref.dtype)


DIFF_ATTN_TQ = 512
DIFF_ATTN_TK = 768


def diff_attention(lamv, q, k, v1, subln_gain, lam_init):
    b, h, s, _ = q.shape
    t = k.shape[2]
    tq = _pick_tile(s, DIFF_ATTN_TQ, 8 * BF16_SUBLANES)
    tk = _pick_tile(t, DIFF_ATTN_TK, LANES)
    stat = lambda w: pltpu.VMEM((2 * tq, w), F32)
    return pl.pallas_call(
        functools.partial(_diff_attn_kernel, tk=tk, lam_init=lam_init),
        grid=(b, h, s // tq),
        in_specs=[pl.BlockSpec((4, HEAD_DIM_A), lambda i, j, qi: (0, 0)),
                  pl.BlockSpec((1, 1, tq, LANES), lambda i, j, qi: (i, j, qi, 0)),
                  pl.BlockSpec((1, 1, t, LANES), lambda i, j, qi: (i, j, 0, 0)),
                  pl.BlockSpec((1, 1, t, 2 * LANES), lambda i, j, qi: (i, j, 0, 0)),
                  pl.BlockSpec((1, LANES), lambda i, j, qi: (0, 0))],
        out_specs=pl.BlockSpec((1, tq, LANES), lambda i, j, qi: (i, qi, j)),
        out_shape=jax.ShapeDtypeStruct((b, s, h * LANES), BF16),
        scratch_shapes=[stat(LANES), stat(2 * LANES), pltpu.VMEM((2 * tq, LANES), BF16), stat(tk), stat(tk)],
        compiler_params=_params("parallel", "parallel", "arbitrary"),
        name="diff_attention",
    )(lamv, q, k, v1, subln_gain.reshape(1, LANES).astype(F32))


def _na_kernel(q_ref, k_ref, v_ref, bias_ref, o_ref, *, n_ctx, rows, kh):
    win = kh * GRID_W
    k_ctx = k_ref[0, 0, 0:n_ctx, :]
    v_ctx = v_ref[0, 0, 0:n_ctx, :]
    dn = (((1,), (1,)), ((), ()))

    def body(r, carry):
        r_start = jnp.clip(r - kh // 2, 0, rows - kh)
        q_r = q_ref[0, 0, pl.ds(pl.multiple_of(r * GRID_W, GRID_W), GRID_W), :]
        k_off = pl.multiple_of(n_ctx + r_start * GRID_W, GRID_W)
        k_w = k_ref[0, 0, pl.ds(k_off, win), :]
        v_w = v_ref[0, 0, pl.ds(k_off, win), :]
        s_lat = lax.dot_general(q_r, k_w, dn, preferred_element_type=F32) + bias_ref[0, r - r_start]
        s_ctx = lax.dot_general(q_r, k_ctx, dn, preferred_element_type=F32)
        m = jnp.maximum(jnp.max(s_lat, axis=-1, keepdims=True), jnp.max(s_ctx, axis=-1, keepdims=True))
        p_lat = jnp.exp(s_lat - m)
        p_ctx = jnp.exp(s_ctx - m)
        den = jnp.sum(p_lat, axis=-1, keepdims=True) + jnp.sum(p_ctx, axis=-1, keepdims=True)
        y = (jnp.dot(p_ctx.astype(BF16), v_ctx, preferred_element_type=F32)
             + jnp.dot(p_lat.astype(BF16), v_w, preferred_element_type=F32))
        o_ref[0, pl.ds(pl.multiple_of(r * GRID_W, GRID_W), GRID_W), :] = (y / den).astype(o_ref.dtype)
        return carry

    lax.fori_loop(0, rows, body, 0, unroll=2)


def na_bias_table(rpb, kh):
    col = jnp.arange(GRID_W, dtype=jnp.int32)
    col_start = jnp.clip(col - WIN_W // 2, 0, GRID_W - WIN_W)
    in_win = (col[None, :] >= col_start[:, None]) & (col[None, :] < col_start[:, None] + WIN_W)
    idx_c = jnp.clip(col[None, :] - col[:, None] + WIN_W - 1, 0, 2 * WIN_W - 2)
    v = jnp.arange(kh, dtype=jnp.int32)
    idx_r = jnp.arange(kh, dtype=jnp.int32)[None, :] - v[:, None] + WIN_H - 1
    pick_r = (idx_r[:, :, None] == jnp.arange(rpb.shape[1])[None, None, :]).astype(F32)
    pick_c = (idx_c[:, :, None] == jnp.arange(rpb.shape[2])[None, None, :]).astype(F32)
    hp = lax.Precision.HIGHEST
    rows_sel = jnp.einsum("vka,hac->hvkc", pick_r, rpb.astype(F32), precision=hp)
    bias = jnp.einsum("hvkc,qjc->hvqkj", rows_sel, pick_c, precision=hp)
    bias = jnp.where(in_win[None, None, :, None, :], bias, NEG_INF)
    return bias.reshape(rpb.shape[0], kh, GRID_W, kh * GRID_W)


def neighbourhood_attention(q, k, v, bias, n_ctx):
    b, h, s, _ = q.shape
    t = k.shape[2]
    rows = s // GRID_W
    kh = min(WIN_H, rows)
    return pl.pallas_call(
        functools.partial(_na_kernel, n_ctx=n_ctx, rows=rows, kh=kh),
        grid=(b, h),
        in_specs=[pl.BlockSpec((1, 1, s, LANES), lambda i, j: (i, j, 0, 0)),
                  pl.BlockSpec((1, 1, t, LANES), lambda i, j: (i, j, 0, 0)),
                  pl.BlockSpec((1, 1, t, LANES), lambda i, j: (i, j, 0, 0)),
                  pl.BlockSpec((1, kh, GRID_W, kh * GRID_W), lambda i, j: (j, 0, 0, 0))],
        out_specs=pl.BlockSpec((1, s, LANES), lambda i, j: (i, 0, j)),
        out_shape=jax.ShapeDtypeStruct((b, s, h * LANES), BF16),
        compiler_params=_params("parallel", "parallel"),
        name="neighbourhood_attention",
    )(q, k, v, bias)


def _attention_branches(x, c, ctx, c_ctx, w_mod, b_mod, g_norm1, w_in, q_gain_a, k_gain_a, lamv, subln_gain,
                        q_gain_b, k_gain_b, rpb, lam_init):
    b, s, d = x.shape
    l = ctx.shape[1]
    t = l + s
    wa, wb = N_HEADS_A * LANES, N_HEADS_B * LANES
    off_qa, off_qb, off_gate = 0, wa, wa + wb
    off_ka = off_gate + 2 * d
    off_va, off_kb, off_vb = off_ka + wa, off_ka + 2 * wa, off_ka + 2 * wa + wb

    n_rows = -(-(b + 1) // 8) * 8
    cc = jnp.zeros((n_rows, d), F32).at[:b].set(c).at[b].set(c_ctx)
    mod = adaln_mod(cc, w_mod, b_mod)
    lat = [mod[:b, i * d:(i + 1) * d] for i in range(6)]
    cmod = [mod[b, i * d:(i + 1) * d] for i in range(6)]

    h = norm_modulate_tokens(ctx, x, g_norm1, lat[0], lat[1], cmod[0], cmod[1])
    proj = matmul_bf16(h.reshape(b * t, d), w_in.astype(BF16)).reshape(b, t, w_in.shape[1])

    tabs = rope_tables(s, l)
    tabs_q = tuple(a[l:] for a in tabs)
    two = lambda g: jnp.tile(g, LANES // HEAD_DIM_A)
    qa = prep_heads(proj, off_qa, N_HEADS_A, l, s, mode="pair", gain=two(q_gain_a), rope_tabs=tabs_q,
                    scale=HEAD_DIM_A ** -0.5 * math.log2(math.e))
    ka = prep_heads(proj, off_ka, N_HEADS_A, 0, t, mode="pair", gain=two(k_gain_a), rope_tabs=tabs)
    va = prep_heads(proj, off_va, N_HEADS_A, 0, t, mode="copy", out_width=2 * LANES)
    qb = prep_heads(proj, off_qb, N_HEADS_B, l, s, mode="full", gain=q_gain_b, scale=HEAD_DIM_B ** -0.5)
    kb = prep_heads(proj, off_kb, N_HEADS_B, 0, t, mode="full", gain=k_gain_b)
    vb = prep_heads(proj, off_vb, N_HEADS_B, 0, t, mode="copy")

    y_a = diff_attention(lamv, qa, ka, va, subln_gain, lam_init)
    kh = min(WIN_H, s // GRID_W)
    y_b = neighbourhood_attention(qb, kb, vb, na_bias_table(rpb, kh), l)
    return lat, proj, y_a, y_b, off_gate


def _merge_kernel(ya_ref, yb_ref, ga_ref, gb_ref, x_ref, wa_ref, wb_ref, wo_ref, g1_ref, gn_ref, sh_ref, sc_ref,
                  wrh_ref, wrl_ref, xo_ref, h2_ref, aff_ref, *, n_exp):
    ta = jnp.dot(ya_ref[0], wa_ref[...], preferred_element_type=F32)
    tb = jnp.dot(yb_ref[0], wb_ref[...], preferred_element_type=F32)
    u = jax.nn.sigmoid(ga_ref[0].astype(F32)) * ta + jax.nn.sigmoid(gb_ref[0].astype(F32)) * tb
    mix = jnp.dot(u.astype(BF16), wo_ref[...], preferred_element_type=F32)
    xn = x_ref[0] + g1_ref[0] * mix
    xo_ref[0] = xn
    ms = jnp.mean(xn * xn, axis=-1, keepdims=True)
    h2 = (xn * lax.rsqrt(ms + EPS) * gn_ref[...]) * (1.0 + sc_ref[0]) + sh_ref[0]
    hi = h2.astype(BF16)
    h2_ref[0] = hi
    lo = (h2 - hi.astype(F32)).astype(BF16)
    logits = (jnp.dot(hi, wrh_ref[...], preferred_element_type=F32)
              + jnp.dot(lo, wrh_ref[...], preferred_element_type=F32)
              + jnp.dot(hi, wrl_ref[...], preferred_element_type=F32))
    valid = lax.broadcasted_iota(jnp.int32, logits.shape, 1) < n_exp
    logits = jnp.where(valid, logits, -jnp.inf)
    p = jnp.exp(logits - jnp.max(logits, axis=-1, keepdims=True))
    aff_ref[0] = p / jnp.sum(p, axis=-1, keepdims=True)


def merge_and_route(y_a, y_b, proj3, off_gate, n_ctx, x, w_a, w_b, w_o, ga1, g_norm2, sh2, sc2, w_router):
    b, s, d = x.shape
    n_exp = w_router.shape[1]
    tm = _pick_tile(math.gcd(s, n_ctx), 256, 8 * BF16_SUBLANES)
    assert off_gate % d == 0 and n_exp <= LANES
    tb, gb = n_ctx // tm, off_gate // d
    wr = jnp.zeros((d, LANES), F32).at[:, :n_exp].set(w_router)
    wr_hi = wr.astype(BF16)
    wr_lo = (wr - wr_hi.astype(F32)).astype(BF16)
    const = lambda shape: pl.BlockSpec(shape, lambda i, t: (0,) * len(shape), pipeline_mode=pl.Buffered(1))
    tok = lambda w: pl.BlockSpec((1, tm, w), lambda i, t: (i, t, 0))
    per_b = pl.BlockSpec((1, 1, d), lambda i, t: (i, 0, 0))
    return pl.pallas_call(
        functools.partial(_merge_kernel, n_exp=n_exp),
        grid=(b, s // tm),
        in_specs=[tok(y_a.shape[2]), tok(y_b.shape[2]),
                  pl.BlockSpec((1, tm, d), lambda i, t: (i, t + tb, gb)),
                  pl.BlockSpec((1, tm, d), lambda i, t: (i, t + tb, gb + 1)),
                  tok(d), const(w_a.shape), const(w_b.shape), const(w_o.shape),
                  per_b, const((1, d)), per_b, per_b, const((d, LANES)), const((d, LANES))],
        out_specs=[tok(d), tok(d), tok(LANES)],
        out_shape=[jax.ShapeDtypeStruct((b, s, d), F32), jax.ShapeDtypeStruct((b, s, d), BF16),
                   jax.ShapeDtypeStruct((b, s, LANES), F32)],
        compiler_params=_params("parallel", "arbitrary"),
        name="merge_route",
    )(y_a, y_b, proj3, proj3, x, w_a.astype(BF16), w_b.astype(BF16), w_o.astype(BF16),
      ga1.reshape(b, 1, d), g_norm2.reshape(1, d), sh2.reshape(b, 1, d), sc2.reshape(b, 1, d), wr_hi, wr_lo)


def _prefix_incl(mask, tri_ref, ones_ref, blk_ref):
    e, nc, _ = mask.shape
    m2 = mask.reshape(e * nc, LANES).astype(BF16)
    within = jnp.dot(m2, tri_ref[...], preferred_element_type=F32)
    row_tot = jnp.dot(m2, ones_ref[...], preferred_element_type=F32)
    before = jnp.dot(blk_ref[...], row_tot.astype(BF16), preferred_element_type=F32)
    return (within + before).reshape(e, nc, LANES), before.reshape(e, nc, LANES)


def _topk_kernel(aff_ref, tri_ref, ones_ref, blk_ref, pos_ref, off_ref, *, cap):
    a = aff_ref[0]
    bits = pltpu.bitcast(a, jnp.int32)

    def count(mask):
        c = jnp.sum(jnp.where(mask, 1.0, 0.0), axis=2, keepdims=True)
        return jnp.sum(c, axis=1, keepdims=True)

    def step(i, cur):
        cand = cur | jnp.left_shift(jnp.int32(1), 30 - i)
        return jnp.where(count(bits >= cand) >= cap, cand, cur)

    thr = lax.fori_loop(0, 31, step, jnp.zeros((a.shape[0], 1, 1), jnp.int32))
    gt = bits > thr
    eq = bits == thr
    need = cap - count(gt)
    eq_f = jnp.where(eq, 1.0, 0.0)
    eq_incl, _ = _prefix_incl(eq_f, tri_ref, ones_ref, blk_ref)
    sel = gt | (eq & ((eq_incl - eq_f) < need))
    sel_f = jnp.where(sel, 1.0, 0.0)
    sel_incl, before = _prefix_incl(sel_f, tri_ref, ones_ref, blk_ref)
    pos_ref[0] = jnp.where(sel, sel_incl - sel_f, -1.0).astype(jnp.int32)
    off_ref[0] = before.astype(jnp.int32)


def expert_choice_select(aff_e, cap):
    b, e, s = aff_e.shape
    nc = s // LANES
    idx = jnp.arange(LANES)
    tri = (idx[:, None] <= idx[None, :]).astype(BF16)
    ones = jnp.ones((LANES, LANES), BF16)
    r = jnp.arange(e * nc)
    blk = ((r[:, None] // nc == r[None, :] // nc) & (r[None, :] % nc < r[:, None] % nc)).astype(BF16)
    blk4 = pl.BlockSpec((1, e, nc, LANES), lambda i: (i, 0, 0, 0))
    const = lambda n: pl.BlockSpec((n, n), lambda i: (0, 0))
    pos, off = pl.pallas_call(
        functools.partial(_topk_kernel, cap=cap),
        grid=(b,),
        in_specs=[blk4, const(LANES), const(LANES), const(e * nc)],
        out_specs=[blk4, blk4],
        out_shape=[jax.ShapeDtypeStruct((b, e, nc, LANES), jnp.int32)] * 2,
        compiler_params=_params("parallel"),
        name="expert_choice_select",
    )(aff_e.reshape(b, e, nc, LANES), tri, ones, blk)
    return pos.reshape(b, e, 1, s), off[..., 0]


SEL_TILE = 2 * LANES
SEL_WIN = 64


def _gather_kernel(lo_ref, nw_ref, h_ref, pos_ref, o_ref, *, eg, nt, n_exp):
    i, g, t = pl.program_id(0), pl.program_id(1), pl.program_id(2)

    @pl.when(t == 0)
    def _():
        o_ref[...] = jnp.zeros(o_ref.shape, o_ref.dtype)

    h = h_ref[0]
    slot = lax.broadcasted_iota(jnp.int32, (SEL_WIN, h.shape[0]), 0)
    for k in range(eg):
        idx = (i * n_exp + g * eg + k) * nt + t
        lo = lo_ref[idx]
        pos = pos_ref[0, k]

        def window(w, carry, k=k, lo=lo, pos=pos):
            base = pl.multiple_of(lo + w * SEL_WIN, BF16_SUBLANES)
            onehot = jnp.where(slot == pos - base, 1.0, 0.0).astype(BF16)
            rows = jnp.dot(onehot, h, preferred_element_type=F32)
            cur = o_ref[0, k, pl.ds(base, SEL_WIN), :]
            o_ref[0, k, pl.ds(base, SEL_WIN), :] = (cur.astype(F32) + rows).astype(o_ref.dtype)
            return carry

        lax.fori_loop(0, nw_ref[idx], window, 0)


def expert_gather(h2, pos, lo_flat, nw_flat, cap, eg=2):
    b, s, d = h2.shape
    n_exp = pos.shape[1]
    tm = SEL_TILE
    nt = s // tm
    cp = cap + SEL_WIN
    return pl.pallas_call(
        functools.partial(_gather_kernel, eg=eg, nt=nt, n_exp=n_exp),
        grid_spec=pltpu.PrefetchScalarGridSpec(
            num_scalar_prefetch=2,
            grid=(b, n_exp // eg, nt),
            in_specs=[pl.BlockSpec((1, tm, d), lambda i, g, t, lo, nw: (i, t, 0)),
                      pl.BlockSpec((1, eg, 1, tm), lambda i, g, t, lo, nw: (i, g, 0, t))],
            out_specs=pl.BlockSpec((1, eg, cp, d), lambda i, g, t, lo, nw: (i, g, 0, 0))),
        out_shape=jax.ShapeDtypeStruct((b, n_exp, cp, d), BF16),
        compiler_params=_params("parallel", "parallel", "arbitrary"),
        name="expert_gather",
    )(lo_flat, nw_flat, h2, pos)


def _ffn_kernel(x_ref, wg_ref, wu_ref, wd_ref, o_ref, acc_ref, *, cap):
    f = pl.program_id(2)
    x = x_ref[0, 0]
    a = jnp.dot(x, wg_ref[0], preferred_element_type=F32)
    u = jnp.dot(x, wu_ref[0], preferred_element_type=F32)
    part = jnp.dot((a * jax.nn.sigmoid(a) * u).astype(BF16), wd_ref[0], preferred_element_type=F32)

    @pl.when(f == 0)
    def _():
        acc_ref[...] = part

    @pl.when(f > 0)
    def _():
        acc_ref[...] += part

    @pl.when(f == pl.num_programs(2) - 1)
    def _():
        o_ref[0, 0, 0:cap, :] = acc_ref[...].astype(o_ref.dtype)
        o_ref[0, 0, cap:, :] = jnp.zeros((o_ref.shape[2] - cap, o_ref.shape[3]), o_ref.dtype)


def expert_ffn(xe, w_gate, w_up, w_down, cap):
    b, n_exp, cp, d = xe.shape
    ff = w_gate.shape[2]
    tf = _pick_tile(ff, 512, LANES)
    return pl.pallas_call(
        functools.partial(_ffn_kernel, cap=cap),
        grid=(n_exp, b, ff // tf),
        in_specs=[pl.BlockSpec((1, 1, cap, d), lambda e, i, f: (i, e, 0, 0)),
                  pl.BlockSpec((1, d, tf), lambda e, i, f: (e, 0, f)),
                  pl.BlockSpec((1, d, tf), lambda e, i, f: (e, 0, f)),
                  pl.BlockSpec((1, tf, d), lambda e, i, f: (e, f, 0))],
        out_specs=pl.BlockSpec((1, 1, cp, d), lambda e, i, f: (i, e, 0, 0)),
        out_shape=jax.ShapeDtypeStruct((b, n_exp, cp, d), BF16),
        scratch_shapes=[pltpu.VMEM((cap, d), F32)],
        compiler_params=_params("parallel", "parallel", "arbitrary"),
        name="expert_ffn",
    )(xe, w_gate, w_up, w_down)


def _combine_kernel(lo_ref, nw_ref, y_hbm, pos_ref, g_ref, x_ref, ga_ref, o_ref, ybuf, yov, acc_ref, sem, osem,
                    *, nt, n_exp):
    i, t = pl.program_id(0), pl.program_id(1)
    tm = x_ref.shape[1]

    def first_window(e):
        lo = pl.multiple_of(lo_ref[(i * n_exp + e) * nt + t], BF16_SUBLANES)
        return pltpu.make_async_copy(y_hbm.at[i, e, pl.ds(lo, SEL_WIN), :],
                                     ybuf.at[pl.ds(e * SEL_WIN, SEL_WIN), :], sem.at[e])

    for e in range(n_exp):
        first_window(e).start()

    slot = lax.broadcasted_iota(jnp.int32, (SEL_WIN, tm), 0)

    def gated_onehot(e, base):
        return jnp.where(slot == pos_ref[0, e] - base, g_ref[0, e], 0.0).astype(BF16)

    dn = (((0,), (0,)), ((), ()))
    sel = jnp.concatenate([gated_onehot(e, lo_ref[(i * n_exp + e) * nt + t]) for e in range(n_exp)], axis=0)
    for e in range(n_exp):
        first_window(e).wait()
    acc_ref[...] = lax.dot_general(sel, ybuf[...], dn, preferred_element_type=F32)

    for e in range(n_exp):
        idx = (i * n_exp + e) * nt + t
        lo = lo_ref[idx]

        def extra(w, carry, e=e, lo=lo):
            base = pl.multiple_of(lo + w * SEL_WIN, BF16_SUBLANES)
            copy = pltpu.make_async_copy(y_hbm.at[i, e, pl.ds(base, SEL_WIN), :], yov, osem.at[0])
            copy.start()
            copy.wait()
            acc_ref[...] += lax.dot_general(gated_onehot(e, base), yov[...], dn, preferred_element_type=F32)
            return carry

        lax.fori_loop(1, nw_ref[idx], extra, 0)

    o_ref[0] = x_ref[0] + ga_ref[0] * acc_ref[...]


def expert_combine(y, pos, gates, lo_flat, nw_flat, x_new, ga2):
    b, s, d = x_new.shape
    n_exp = pos.shape[1]
    tm = SEL_TILE
    nt = s // tm
    sel = pl.BlockSpec((1, n_exp, 1, tm), lambda i, t, lo, nw: (i, 0, 0, t))
    return pl.pallas_call(
        functools.partial(_combine_kernel, nt=nt, n_exp=n_exp),
        grid_spec=pltpu.PrefetchScalarGridSpec(
            num_scalar_prefetch=2,
            grid=(b, nt),
            in_specs=[pl.BlockSpec(memory_space=pl.ANY), sel, sel,
                      pl.BlockSpec((1, tm, d), lambda i, t, lo, nw: (i, t, 0)),
                      pl.BlockSpec((1, 1, d), lambda i, t, lo, nw: (i, 0, 0))],
            out_specs=pl.BlockSpec((1, tm, d), lambda i, t, lo, nw: (i, t, 0)),
            scratch_shapes=[pltpu.VMEM((n_exp * SEL_WIN, d), BF16), pltpu.VMEM((SEL_WIN, d), BF16),
                            pltpu.VMEM((tm, d), F32), pltpu.SemaphoreType.DMA((n_exp,)),
                            pltpu.SemaphoreType.DMA((1,))]),
        out_shape=jax.ShapeDtypeStruct((b, s, d), F32),
        compiler_params=_params("parallel", "arbitrary"),
        name="expert_combine",
    )(lo_flat, nw_flat, y, pos, gates, x_new, ga2.reshape(b, 1, d))


def expert_choice_moe(x_new, h2, aff, ga2, w_gate, w_up, w_down):
    b, s, d = x_new.shape
    n_exp = w_gate.shape[0]
    cap = CAPACITY_FACTOR * s // n_exp
    aff_e = aff[:, :, :n_exp].transpose(0, 2, 1)
    pos, off = expert_choice_select(aff_e, cap)
    start = off[:, :, ::SEL_TILE // LANES]
    count = jnp.diff(start, axis=-1, append=jnp.full((b, n_exp, 1), cap, start.dtype))
    lo = (start // BF16_SUBLANES) * BF16_SUBLANES
    nw = jnp.where(count > 0, (start - lo + count + SEL_WIN - 1) // SEL_WIN, 0)
    lo_flat, nw_flat = lo.reshape(-1), nw.reshape(-1)
    xe = expert_gather(h2, pos, lo_flat, nw_flat, cap)
    y = expert_ffn(xe, w_gate.astype(BF16), w_up.astype(BF16), w_down.astype(BF16), cap)
    return expert_combine(y, pos, aff_e.reshape(b, n_exp, 1, s), lo_flat, nw_flat, x_new, ga2)


def kernel(x, c, ctx, c_ctx, w_mod, b_mod, g_norm1, g_norm2, w_in, q_gain_a, k_gain_a, lam_q1, lam_k1, lam_q2,
           lam_k2, subln_gain, q_gain_b, k_gain_b, rel_pos_bias, w_branch_a, w_branch_b, w_out, w_router,
           w_exp_gate, w_exp_up, w_exp_down):
    assert w_mod.shape[0] == 1, "single-layer block"
    lam_init = 0.8 - 0.6 * math.exp(-0.3 * 0)
    lamv = jnp.stack([lam_q1[0], lam_k1[0], lam_q2[0], lam_k2[0]]).astype(F32)
    lat, proj, y_a, y_b, off_gate = _attention_branches(
        x, c, ctx, c_ctx, w_mod[0], b_mod[0], g_norm1[0], w_in[0], q_gain_a[0], k_gain_a[0], lamv, subln_gain[0],
        q_gain_b[0], k_gain_b[0], rel_pos_bias[0], lam_init)
    _, _, ga1, sh2, sc2, ga2 = lat
    x_new, h2, aff = merge_and_route(y_a, y_b, proj, off_gate, ctx.shape[1], x, w_branch_a[0], w_branch_b[0],
                                     w_out[0], ga1, g_norm2[0], sh2, sc2, w_router[0])
    return expert_choice_moe(x_new, h2, aff, ga2, w_exp_gate[0], w_exp_up[0], w_exp_down[0])
```

```python
import functools
import math

import jax
import jax.numpy as jnp
from jax import lax
from jax.experimental import pallas as pl
from jax.experimental.pallas import tpu as pltpu

GRID_W = 64
N_HEADS_A = 8
HEAD_DIM_A = 64
N_HEADS_B = 8
HEAD_DIM_B = 128
WIN_H = 8
WIN_W = 16
CAPACITY_FACTOR = 2
ROPE_BASE = 10000.0
EPS = 1e-6
NEG_INF = -1e30

LANES = 128
BF16_SUBLANES = 16
VMEM_LIMIT_BYTES = 56 * 1024 * 1024

F32 = jnp.float32
BF16 = jnp.bfloat16


def _params(*sem):
    return pltpu.CompilerParams(dimension_semantics=sem, vmem_limit_bytes=VMEM_LIMIT_BYTES)


def _pick_tile(n, cap, mult):
    best = None
    for t in range(mult, min(n, cap) + 1, mult):
        if n % t == 0:
            best = t
    assert best is not None, (n, cap, mult)
    return best


def _mod_kernel(cc_ref, w_ref, b_ref, o_ref):
    cc = cc_ref[...]
    act = cc * jax.nn.sigmoid(cc)
    o_ref[...] = jnp.dot(act, w_ref[...], preferred_element_type=F32,
                         precision=lax.Precision.HIGHEST) + b_ref[...]


def adaln_mod(cc, w_mod, b_mod):
    r, d = cc.shape
    n = w_mod.shape[1]
    tn = _pick_tile(n, 1024, LANES)
    return pl.pallas_call(
        _mod_kernel,
        grid=(n // tn,),
        in_specs=[pl.BlockSpec((r, d), lambda j: (0, 0)),
                  pl.BlockSpec((d, tn), lambda j: (0, j)),
                  pl.BlockSpec((1, tn), lambda j: (0, j))],
        out_specs=pl.BlockSpec((r, tn), lambda j: (0, j)),
        out_shape=jax.ShapeDtypeStruct((r, n), F32),
        compiler_params=_params("parallel"),
        name="adaln_mod",
    )(cc, w_mod, b_mod.reshape(1, n))


def _norm_mod_kernel(ctx_ref, x_ref, g_ref, sh_ref, sc_ref, csh_ref, csc_ref, o_ref):
    def emit(src_ref, shift, scale):
        xf = src_ref[0]
        ms = jnp.mean(xf * xf, axis=-1, keepdims=True)
        y = xf * lax.rsqrt(ms + EPS) * g_ref[...]
        o_ref[0] = (y * (1.0 + scale) + shift).astype(o_ref.dtype)

    t = pl.program_id(1)

    @pl.when(t == 0)
    def _():
        emit(ctx_ref, csh_ref[...], csc_ref[...])

    @pl.when(t > 0)
    def _():
        emit(x_ref, sh_ref[0], sc_ref[0])


def norm_modulate_tokens(ctx, x, g, sh, sc, csh, csc):
    b, s, d = x.shape
    l = ctx.shape[1]
    assert s % l == 0
    nt = (l + s) // l
    vec = lambda a: a.reshape(1, d)
    return pl.pallas_call(
        _norm_mod_kernel,
        grid=(b, nt),
        in_specs=[pl.BlockSpec((1, l, d), lambda i, t: (i, 0, 0)),
                  pl.BlockSpec((1, l, d), lambda i, t: (i, jnp.maximum(t - 1, 0), 0)),
                  pl.BlockSpec((1, d), lambda i, t: (0, 0)),
                  pl.BlockSpec((1, 1, d), lambda i, t: (i, 0, 0)),
                  pl.BlockSpec((1, 1, d), lambda i, t: (i, 0, 0)),
                  pl.BlockSpec((1, d), lambda i, t: (0, 0)),
                  pl.BlockSpec((1, d), lambda i, t: (0, 0))],
        out_specs=pl.BlockSpec((1, l, d), lambda i, t: (i, t, 0)),
        out_shape=jax.ShapeDtypeStruct((b, l + s, d), BF16),
        compiler_params=_params("parallel", "arbitrary"),
        name="norm_modulate",
    )(ctx, x, vec(g), sh.reshape(b, 1, d), sc.reshape(b, 1, d), vec(csh), vec(csc))


def _matmul_kernel(x_ref, w_ref, o_ref):
    o_ref[...] = jnp.dot(x_ref[...], w_ref[...], preferred_element_type=F32).astype(o_ref.dtype)


def matmul_bf16(x, w):
    m, k = x.shape
    n = w.shape[1]
    tm = _pick_tile(m, 1024, 8 * BF16_SUBLANES)
    tn = _pick_tile(n, 1024, LANES)
    return pl.pallas_call(
        _matmul_kernel,
        grid=(m // tm, n // tn),
        in_specs=[pl.BlockSpec((tm, k), lambda i, j: (i, 0)),
                  pl.BlockSpec((k, tn), lambda i, j: (0, j))],
        out_specs=pl.BlockSpec((tm, tn), lambda i, j: (i, j)),
        out_shape=jax.ShapeDtypeStruct((m, n), BF16),
        compiler_params=_params("parallel", "arbitrary"),
        name="in_proj",
    )(x, w)


def _prep_kernel(*refs, n_heads, mode, rope, scale):
    if rope:
        x_ref, gain_ref, cos_ref, sa_ref, sb_ref, o_ref = refs
    elif mode != "copy":
        x_ref, gain_ref, o_ref = refs
    else:
        x_ref, o_ref = refs
    for h in range(n_heads):
        x = x_ref[0, :, h * LANES:(h + 1) * LANES]
        if mode == "copy":
            o_ref[0, h, :, 0:LANES] = x
            if o_ref.shape[3] > LANES:
                o_ref[0, h, :, LANES:] = jnp.ones((x.shape[0], o_ref.shape[3] - LANES), o_ref.dtype)
            continue
        x = x.astype(F32)
        x2 = x * x
        if mode == "pair":
            lo = lax.broadcasted_iota(jnp.int32, x.shape, 1) < HEAD_DIM_A
            s_lo = jnp.sum(jnp.where(lo, x2, 0.0), axis=-1, keepdims=True)
            s_hi = jnp.sum(jnp.where(lo, 0.0, x2), axis=-1, keepdims=True)
            r = jnp.where(lo, lax.rsqrt(s_lo * (1.0 / HEAD_DIM_A) + EPS),
                          lax.rsqrt(s_hi * (1.0 / HEAD_DIM_A) + EPS))
        else:
            r = lax.rsqrt(jnp.mean(x2, axis=-1, keepdims=True) + EPS)
        x = x * r * gain_ref[...]
        if rope:
            x = (x * cos_ref[...] + pltpu.roll(x, LANES - 16, 1) * sa_ref[...]
                 + pltpu.roll(x, 16, 1) * sb_ref[...])
        if scale != 1.0:
            x = x * scale
        o_ref[0, h] = x.astype(o_ref.dtype)


def prep_heads(proj3, col_off, n_heads, tok_off, n_tok, *, mode, gain=None, rope_tabs=None, scale=1.0,
               out_width=LANES):
    b, t_all, _ = proj3.shape
    width = n_heads * LANES
    assert col_off % width == 0
    tt = _pick_tile(math.gcd(n_tok, tok_off) if tok_off else n_tok, 512, 8 * BF16_SUBLANES)
    assert tok_off % tt == 0 and n_tok % tt == 0
    cb, tb = col_off // width, tok_off // tt
    in_specs = [pl.BlockSpec((1, tt, width), lambda ti, i: (i, ti + tb, cb))]
    args = [proj3]
    if mode != "copy":
        in_specs.append(pl.BlockSpec((1, LANES), lambda ti, i: (0, 0)))
        args.append(gain.reshape(1, LANES).astype(F32))
    if rope_tabs is not None:
        for tab in rope_tabs:
            in_specs.append(pl.BlockSpec((tt, LANES), lambda ti, i: (ti, 0)))
            args.append(tab)
    return pl.pallas_call(
        functools.partial(_prep_kernel, n_heads=n_heads, mode=mode, rope=rope_tabs is not None, scale=scale),
        grid=(n_tok // tt, b),
        in_specs=in_specs,
        out_specs=pl.BlockSpec((1, n_heads, tt, out_width), lambda ti, i: (i, 0, ti, 0)),
        out_shape=jax.ShapeDtypeStruct((b, n_heads, n_tok, out_width), BF16),
        compiler_params=_params("parallel", "arbitrary"),
        name="prep_" + mode,
    )(*args)


def rope_tables(n_lat, n_ctx):
    t = jnp.arange(n_lat, dtype=jnp.int32)
    row = (t // GRID_W).astype(F32)
    col = (t % GRID_W).astype(F32)
    half = HEAD_DIM_A // 2
    inv_freq = ROPE_BASE ** (-jnp.arange(0, half, 2, dtype=F32) / half)

    def tab(pos):
        ang = pos[:, None] * inv_freq[None, :]
        ang = jnp.concatenate([ang, ang], axis=-1)
        return jnp.cos(ang), jnp.sin(ang)

    cr, sr = tab(row)
    cc, sc = tab(col)
    cos = jnp.tile(jnp.concatenate([cr, cc], axis=-1), (1, LANES // HEAD_DIM_A))
    sin = jnp.tile(jnp.concatenate([sr, sc], axis=-1), (1, LANES // HEAD_DIM_A))
    first = (jnp.arange(LANES) % (half)) < (half // 2)
    sa = jnp.where(first[None, :], -sin, 0.0)
    sb = jnp.where(first[None, :], 0.0, sin)
    pad = lambda a, v: jnp.concatenate([jnp.full((n_ctx, LANES), v, F32), a], axis=0)
    return pad(cos, 1.0), pad(sa, 0.0), pad(sb, 0.0)


def _diff_attn_kernel(lamv_ref, q_ref, k_ref, v_ref, sg_ref, o_ref, m_s, a_s, qq_s, sa, sb, *, tk, lam_init):
    n_k = k_ref.shape[2] // tk
    tq = q_ref.shape[2]
    q = q_ref[0, 0]
    lo = lax.broadcasted_iota(jnp.int32, q.shape, 1) < HEAD_DIM_A
    zero = jnp.zeros_like(q)
    qq_s[0:tq, :] = jnp.where(lo, q, zero)
    qq_s[tq:, :] = jnp.where(lo, zero, q)
    m_s[...] = jnp.full(m_s.shape, -jnp.inf, F32)
    a_s[...] = jnp.zeros(a_s.shape, F32)

    def produce(c, s_ref):
        k = k_ref[0, 0, pl.ds(pl.multiple_of(c * tk, tk), tk), :]
        s_ref[...] = lax.dot_general(qq_s[...], k, (((1,), (1,)), ((), ())), preferred_element_type=F32)

    def consume(c, s_ref):
        v = v_ref[0, 0, pl.ds(pl.multiple_of(c * tk, tk), tk), :]
        s = s_ref[...]
        m_old = m_s[...]
        m_new = jnp.maximum(m_old, jnp.max(s, axis=-1, keepdims=True))
        alpha = jnp.exp2(m_old - m_new)
        p = jnp.exp2(s - jnp.tile(m_new, (1, tk // LANES)))
        a_s[...] = jnp.tile(alpha, (1, 2)) * a_s[...] + jnp.dot(p.astype(BF16), v, preferred_element_type=F32)
        m_s[...] = m_new

    bufs = (sa, sb)
    produce(0, sa)
    for c in range(n_k):
        if c + 1 < n_k:
            produce(c + 1, bufs[(c + 1) % 2])
        consume(c, bufs[c % 2])

    lv = lamv_ref[...]
    lam = (jnp.exp(jnp.sum(lv[0:1] * lv[1:2], axis=-1, keepdims=True))
           - jnp.exp(jnp.sum(lv[2:3] * lv[3:4], axis=-1, keepdims=True)) + lam_init)
    y = (a_s[0:tq, 0:LANES] / a_s[0:tq, LANES:] - lam * (a_s[tq:, 0:LANES] / a_s[tq:, LANES:]))
    y = y * lax.rsqrt(jnp.mean(y * y, axis=-1, keepdims=True) + EPS) * sg_ref[...]
    o_ref[0] = (y * (1.0 - lam_init)).astype(o_ref.dtype)


DIFF_ATTN_TQ = 512
DIFF_ATTN_TK = 768


def diff_attention(lamv, q, k, v1, subln_gain, lam_init):
    b, h, s, _ = q.shape
    t = k.shape[2]
    tq = _pick_tile(s, DIFF_ATTN_TQ, 8 * BF16_SUBLANES)
    tk = _pick_tile(t, DIFF_ATTN_TK, LANES)
    stat = lambda w: pltpu.VMEM((2 * tq, w), F32)
    return pl.pallas_call(
        functools.partial(_diff_attn_kernel, tk=tk, lam_init=lam_init),
        grid=(b, h, s // tq),
        in_specs=[pl.BlockSpec((4, HEAD_DIM_A), lambda i, j, qi: (0, 0)),
                  pl.BlockSpec((1, 1, tq, LANES), lambda i, j, qi: (i, j, qi, 0)),
                  pl.BlockSpec((1, 1, t, LANES), lambda i, j, qi: (i, j, 0, 0)),
                  pl.BlockSpec((1, 1, t, 2 * LANES), lambda i, j, qi: (i, j, 0, 0)),
                  pl.BlockSpec((1, LANES), lambda i, j, qi: (0, 0))],
        out_specs=pl.BlockSpec((1, tq, LANES), lambda i, j, qi: (i, qi, j)),
        out_shape=jax.ShapeDtypeStruct((b, s, h * LANES), BF16),
        scratch_shapes=[stat(LANES), stat(2 * LANES), pltpu.VMEM((2 * tq, LANES), BF16), stat(tk), stat(tk)],
        compiler_params=_params("parallel", "parallel", "arbitrary"),
        name="diff_attention",
    )(lamv, q, k, v1, subln_gain.reshape(1, LANES).astype(F32))


NA_QROWS = 8
NA_KROWS = NA_QROWS + WIN_H
ROWS_PER_VREG = LANES // GRID_W


def _na_kernel(q_ref, k_ref, v_ref, bias_ref, o_ref, sa_ref, sb_ref, *, n_ctx, rows):
    nq, nkw = NA_QROWS * GRID_W, NA_KROWS * GRID_W
    masked = bias_ref.shape[2] - 1
    k_ctx = k_ref[0, 0, 0:n_ctx, :]
    v_ctx = v_ref[0, 0, 0:n_ctx, :]
    dn = (((1,), (1,)), ((), ()))

    def block(j, s_ref):
        r0 = j * NA_QROWS
        w0 = jnp.clip(r0 - WIN_H // 2, 0, rows - NA_KROWS)
        q_off = pl.multiple_of(r0 * GRID_W, nq)
        k_off = pl.multiple_of(n_ctx + w0 * GRID_W, GRID_W)
        q = q_ref[0, 0, pl.ds(q_off, nq), :]
        s_ref[:, 0:n_ctx] = lax.dot_general(q, k_ctx, dn, preferred_element_type=F32)
        s_ref[:, n_ctx:] = lax.dot_general(q, k_ref[0, 0, pl.ds(k_off, nkw), :], dn, preferred_element_type=F32)
        for i in range(NA_QROWS):
            r = r0 + i
            below = r - jnp.clip(r - WIN_H // 2, 0, rows - WIN_H)
            for p in range(NA_KROWS // ROWS_PER_VREG):
                a0 = w0 + p * ROWS_PER_VREG - r + (WIN_H - 1)
                touches = (a0 >= WIN_H - ROWS_PER_VREG - below) & (a0 < 2 * WIN_H - 1 - below)
                entry = jnp.where(touches, a0 + ROWS_PER_VREG - 1, masked)
                cols = slice(n_ctx + p * LANES, n_ctx + (p + 1) * LANES)
                s_ref[i * GRID_W:(i + 1) * GRID_W, cols] += bias_ref[0, below, entry]
        s = s_ref[...]
        p = jnp.exp2(s - jnp.max(s, axis=-1, keepdims=True)).astype(BF16)
        acc = (jnp.dot(p[:, 0:n_ctx], v_ctx, preferred_element_type=F32)
               + jnp.dot(p[:, n_ctx:], v_ref[0, 0, pl.ds(k_off, nkw), :], preferred_element_type=F32))
        o_ref[0, pl.ds(q_off, nq), :] = (acc[:, 0:LANES] / acc[:, LANES:]).astype(o_ref.dtype)

    def body(jj, carry):
        block(2 * jj, sa_ref)
        block(2 * jj + 1, sb_ref)
        return carry

    lax.fori_loop(0, rows // (2 * NA_QROWS), body, 0)


def na_bias_table(rpb):
    n_rel = 2 * WIN_H - 1
    col = jnp.arange(GRID_W, dtype=jnp.int32)
    col_start = jnp.clip(col - WIN_W // 2, 0, GRID_W - WIN_W)
    in_win = (col[None, :] >= col_start[:, None]) & (col[None, :] < col_start[:, None] + WIN_W)
    idx_c = jnp.clip(col[None, :] - col[:, None] + WIN_W - 1, 0, 2 * WIN_W - 2)
    hp = lax.Precision.HIGHEST
    pick_c = (idx_c[:, :, None] == jnp.arange(rpb.shape[2])[None, None, :]).astype(F32)
    toeplitz = jnp.einsum("hac,qkc->haqk", rpb.astype(F32), pick_c, precision=hp)
    toeplitz = jnp.where(in_win[None, None], toeplitz, NEG_INF)
    below = jnp.arange(WIN_H)[:, None, None]
    entry = jnp.arange(n_rel + ROWS_PER_VREG - 1)[None, :, None]
    rel = entry - (ROWS_PER_VREG - 1) + jnp.arange(ROWS_PER_VREG)[None, None, :]
    valid = (rel >= WIN_H - 1 - below) & (rel < n_rel - below)
    pick_r = ((rel[..., None] == jnp.arange(n_rel)) & valid[..., None]).astype(F32)
    tiles = jnp.einsum("vega,haqk->hveqgk", pick_r, jnp.where(in_win[None, None], toeplitz, 0.0), precision=hp)
    keep = valid[None, :, :, None, :, None] & in_win[None, None, None, :, None, :]
    tiles = jnp.where(keep, tiles, NEG_INF)
    tiles = jnp.concatenate([tiles, jnp.full_like(tiles[:, :, :1], NEG_INF)], axis=2)
    h, _, n_e = tiles.shape[:3]
    return (tiles * math.log2(math.e)).reshape(h, WIN_H, n_e, GRID_W, LANES)


def neighbourhood_attention(q, k, v1, bias, n_ctx):
    b, h, s, _ = q.shape
    t = k.shape[2]
    rows = s // GRID_W
    assert rows % (2 * NA_QROWS) == 0 and rows >= NA_KROWS and LANES % GRID_W == 0
    nq = NA_QROWS * GRID_W
    return pl.pallas_call(
        functools.partial(_na_kernel, n_ctx=n_ctx, rows=rows),
        grid=(b, h),
        in_specs=[pl.BlockSpec((1, 1, s, LANES), lambda i, j: (i, j, 0, 0)),
                  pl.BlockSpec((1, 1, t, LANES), lambda i, j: (i, j, 0, 0)),
                  pl.BlockSpec((1, 1, t, 2 * LANES), lambda i, j: (i, j, 0, 0)),
                  pl.BlockSpec((1,) + bias.shape[1:], lambda i, j: (j, 0, 0, 0, 0))],
        out_specs=pl.BlockSpec((1, s, LANES), lambda i, j: (i, 0, j)),
        out_shape=jax.ShapeDtypeStruct((b, s, h * LANES), BF16),
        scratch_shapes=[pltpu.VMEM((nq, n_ctx + NA_KROWS * GRID_W), F32)] * 2,
        compiler_params=_params("parallel", "parallel"),
        name="neighbourhood_attention",
    )(q, k, v1, bias)


def _attention_branches(x, c, ctx, c_ctx, w_mod, b_mod, g_norm1, w_in, q_gain_a, k_gain_a, lamv, subln_gain,
                        q_gain_b, k_gain_b, rpb, lam_init):
    b, s, d = x.shape
    l = ctx.shape[1]
    t = l + s
    wa, wb = N_HEADS_A * LANES, N_HEADS_B * LANES
    off_qa, off_qb, off_gate = 0, wa, wa + wb
    off_ka = off_gate + 2 * d
    off_va, off_kb, off_vb = off_ka + wa, off_ka + 2 * wa, off_ka + 2 * wa + wb

    n_rows = -(-(b + 1) // 8) * 8
    cc = jnp.zeros((n_rows, d), F32).at[:b].set(c).at[b].set(c_ctx)
    mod = adaln_mod(cc, w_mod, b_mod)
    lat = [mod[:b, i * d:(i + 1) * d] for i in range(6)]
    cmod = [mod[b, i * d:(i + 1) * d] for i in range(6)]

    h = norm_modulate_tokens(ctx, x, g_norm1, lat[0], lat[1], cmod[0], cmod[1])
    proj = matmul_bf16(h.reshape(b * t, d), w_in.astype(BF16)).reshape(b, t, w_in.shape[1])

    tabs = rope_tables(s, l)
    tabs_q = tuple(a[l:] for a in tabs)
    two = lambda g: jnp.tile(g, LANES // HEAD_DIM_A)
    qa = prep_heads(proj, off_qa, N_HEADS_A, l, s, mode="pair", gain=two(q_gain_a), rope_tabs=tabs_q,
                    scale=HEAD_DIM_A ** -0.5 * math.log2(math.e))
    ka = prep_heads(proj, off_ka, N_HEADS_A, 0, t, mode="pair", gain=two(k_gain_a), rope_tabs=tabs)
    va = prep_heads(proj, off_va, N_HEADS_A, 0, t, mode="copy", out_width=2 * LANES)
    qb = prep_heads(proj, off_qb, N_HEADS_B, l, s, mode="full", gain=q_gain_b,
                    scale=HEAD_DIM_B ** -0.5 * math.log2(math.e))
    kb = prep_heads(proj, off_kb, N_HEADS_B, 0, t, mode="full", gain=k_gain_b)
    vb = prep_heads(proj, off_vb, N_HEADS_B, 0, t, mode="copy", out_width=2 * LANES)

    y_a = diff_attention(lamv, qa, ka, va, subln_gain, lam_init)
    y_b = neighbourhood_attention(qb, kb, vb, na_bias_table(rpb), l)
    return lat, proj, y_a, y_b, off_gate


def _merge_kernel(ya_ref, yb_ref, ga_ref, gb_ref, x_ref, wa_ref, wb_ref, wo_ref, g1_ref, gn_ref, sh_ref, sc_ref,
                  wrh_ref, wrl_ref, xo_ref, h2_ref, aff_ref, *, n_exp):
    ta = jnp.dot(ya_ref[0], wa_ref[...], preferred_element_type=F32)
    tb = jnp.dot(yb_ref[0], wb_ref[...], preferred_element_type=F32)
    u = jax.nn.sigmoid(ga_ref[0].astype(F32)) * ta + jax.nn.sigmoid(gb_ref[0].astype(F32)) * tb
    mix = jnp.dot(u.astype(BF16), wo_ref[...], preferred_element_type=F32)
    xn = x_ref[0] + g1_ref[0] * mix
    xo_ref[0] = xn
    ms = jnp.mean(xn * xn, axis=-1, keepdims=True)
    h2 = (xn * lax.rsqrt(ms + EPS) * gn_ref[...]) * (1.0 + sc_ref[0]) + sh_ref[0]
    hi = h2.astype(BF16)
    h2_ref[0] = hi
    lo = (h2 - hi.astype(F32)).astype(BF16)
    logits = (jnp.dot(hi, wrh_ref[...], preferred_element_type=F32)
              + jnp.dot(lo, wrh_ref[...], preferred_element_type=F32)
              + jnp.dot(hi, wrl_ref[...], preferred_element_type=F32))
    valid = lax.broadcasted_iota(jnp.int32, logits.shape, 1) < n_exp
    logits = jnp.where(valid, logits, -jnp.inf)
    p = jnp.exp(logits - jnp.max(logits, axis=-1, keepdims=True))
    aff_ref[0] = p / jnp.sum(p, axis=-1, keepdims=True)


def merge_and_route(y_a, y_b, proj3, off_gate, n_ctx, x, w_a, w_b, w_o, ga1, g_norm2, sh2, sc2, w_router):
    b, s, d = x.shape
    n_exp = w_router.shape[1]
    tm = _pick_tile(math.gcd(s, n_ctx), 256, 8 * BF16_SUBLANES)
    assert off_gate % d == 0 and n_exp <= LANES
    tb, gb = n_ctx // tm, off_gate // d
    wr = jnp.zeros((d, LANES), F32).at[:, :n_exp].set(w_router)
    wr_hi = wr.astype(BF16)
    wr_lo = (wr - wr_hi.astype(F32)).astype(BF16)
    const = lambda shape: pl.BlockSpec(shape, lambda i, t: (0,) * len(shape), pipeline_mode=pl.Buffered(1))
    tok = lambda w: pl.BlockSpec((1, tm, w), lambda i, t: (i, t, 0))
    per_b = pl.BlockSpec((1, 1, d), lambda i, t: (i, 0, 0))
    return pl.pallas_call(
        functools.partial(_merge_kernel, n_exp=n_exp),
        grid=(b, s // tm),
        in_specs=[tok(y_a.shape[2]), tok(y_b.shape[2]),
                  pl.BlockSpec((1, tm, d), lambda i, t: (i, t + tb, gb)),
                  pl.BlockSpec((1, tm, d), lambda i, t: (i, t + tb, gb + 1)),
                  tok(d), const(w_a.shape), const(w_b.shape), const(w_o.shape),
                  per_b, const((1, d)), per_b, per_b, const((d, LANES)), const((d, LANES))],
        out_specs=[tok(d), tok(d), tok(LANES)],
        out_shape=[jax.ShapeDtypeStruct((b, s, d), F32), jax.ShapeDtypeStruct((b, s, d), BF16),
                   jax.ShapeDtypeStruct((b, s, LANES), F32)],
        compiler_params=_params("parallel", "arbitrary"),
        name="merge_route",
    )(y_a, y_b, proj3, proj3, x, w_a.astype(BF16), w_b.astype(BF16), w_o.astype(BF16),
      ga1.reshape(b, 1, d), g_norm2.reshape(1, d), sh2.reshape(b, 1, d), sc2.reshape(b, 1, d), wr_hi, wr_lo)


def _prefix_incl(mask, tri_ref, ones_ref, blk_ref):
    e, nc, _ = mask.shape
    m2 = mask.reshape(e * nc, LANES).astype(BF16)
    within = jnp.dot(m2, tri_ref[...], preferred_element_type=F32)
    row_tot = jnp.dot(m2, ones_ref[...], preferred_element_type=F32)
    before = jnp.dot(blk_ref[...], row_tot.astype(BF16), preferred_element_type=F32)
    return (within + before).reshape(e, nc, LANES), before.reshape(e, nc, LANES)


def _topk_kernel(aff_ref, tri_ref, ones_ref, blk_ref, pos_ref, off_ref, *, cap):
    a = aff_ref[0]
    bits = pltpu.bitcast(a, jnp.int32)

    def count(mask):
        c = jnp.sum(jnp.where(mask, 1.0, 0.0), axis=2, keepdims=True)
        return jnp.sum(c, axis=1, keepdims=True)

    def step(i, cur):
        cand = cur | jnp.left_shift(jnp.int32(1), 30 - i)
        return jnp.where(count(bits >= cand) >= cap, cand, cur)

    thr = lax.fori_loop(0, 31, step, jnp.zeros((a.shape[0], 1, 1), jnp.int32))
    gt = bits > thr
    eq = bits == thr
    need = cap - count(gt)
    eq_f = jnp.where(eq, 1.0, 0.0)
    eq_incl, _ = _prefix_incl(eq_f, tri_ref, ones_ref, blk_ref)
    sel = gt | (eq & ((eq_incl - eq_f) < need))
    sel_f = jnp.where(sel, 1.0, 0.0)
    sel_incl, before = _prefix_incl(sel_f, tri_ref, ones_ref, blk_ref)
    pos_ref[0] = jnp.where(sel, sel_incl - sel_f, -1.0).astype(jnp.int32)
    off_ref[0] = before.astype(jnp.int32)


def expert_choice_select(aff_e, cap):
    b, e, s = aff_e.shape
    nc = s // LANES
    idx = jnp.arange(LANES)
    tri = (idx[:, None] <= idx[None, :]).astype(BF16)
    ones = jnp.ones((LANES, LANES), BF16)
    r = jnp.arange(e * nc)
    blk = ((r[:, None] // nc == r[None, :] // nc) & (r[None, :] % nc < r[:, None] % nc)).astype(BF16)
    blk4 = pl.BlockSpec((1, e, nc, LANES), lambda i: (i, 0, 0, 0))
    const = lambda n: pl.BlockSpec((n, n), lambda i: (0, 0))
    pos, off = pl.pallas_call(
        functools.partial(_topk_kernel, cap=cap),
        grid=(b,),
        in_specs=[blk4, const(LANES), const(LANES), const(e * nc)],
        out_specs=[blk4, blk4],
        out_shape=[jax.ShapeDtypeStruct((b, e, nc, LANES), jnp.int32)] * 2,
        compiler_params=_params("parallel"),
        name="expert_choice_select",
    )(aff_e.reshape(b, e, nc, LANES), tri, ones, blk)
    return pos.reshape(b, e, 1, s), off[..., 0]


SEL_TILE = 2 * LANES
SEL_WIN = 64


def _gather_kernel(lo_ref, nw_ref, h_ref, pos_ref, o_ref, *, eg, nt, n_exp):
    i, g, t = pl.program_id(0), pl.program_id(1), pl.program_id(2)

    @pl.when(t == 0)
    def _():
        o_ref[...] = jnp.zeros(o_ref.shape, o_ref.dtype)

    h = h_ref[0]
    slot = lax.broadcasted_iota(jnp.int32, (SEL_WIN, h.shape[0]), 0)

    def onehot(k, base):
        return jnp.where(slot == pos_ref[0, k] - base, 1.0, 0.0).astype(BF16)

    def add_rows(k, base, rows):
        cur = o_ref[0, k, pl.ds(base, SEL_WIN), :]
        o_ref[0, k, pl.ds(base, SEL_WIN), :] = (cur.astype(F32) + rows).astype(o_ref.dtype)

    idx = [(i * n_exp + g * eg + k) * nt + t for k in range(eg)]
    lo = [pl.multiple_of(lo_ref[idx[k]], BF16_SUBLANES) for k in range(eg)]
    first = jnp.dot(jnp.concatenate([onehot(k, lo[k]) for k in range(eg)], axis=0), h,
                    preferred_element_type=F32)
    for k in range(eg):
        add_rows(k, lo[k], first[k * SEL_WIN:(k + 1) * SEL_WIN])

        def extra(w, carry, k=k):
            base = pl.multiple_of(lo[k] + w * SEL_WIN, BF16_SUBLANES)
            add_rows(k, base, jnp.dot(onehot(k, base), h, preferred_element_type=F32))
            return carry

        lax.fori_loop(1, nw_ref[idx[k]], extra, 0)


def expert_gather(h2, pos, lo_flat, nw_flat, cap, eg=4):
    b, s, d = h2.shape
    n_exp = pos.shape[1]
    tm = SEL_TILE
    nt = s // tm
    cp = cap + SEL_WIN
    return pl.pallas_call(
        functools.partial(_gather_kernel, eg=eg, nt=nt, n_exp=n_exp),
        grid_spec=pltpu.PrefetchScalarGridSpec(
            num_scalar_prefetch=2,
            grid=(b, n_exp // eg, nt),
            in_specs=[pl.BlockSpec((1, tm, d), lambda i, g, t, lo, nw: (i, t, 0)),
                      pl.BlockSpec((1, eg, 1, tm), lambda i, g, t, lo, nw: (i, g, 0, t))],
            out_specs=pl.BlockSpec((1, eg, cp, d), lambda i, g, t, lo, nw: (i, g, 0, 0))),
        out_shape=jax.ShapeDtypeStruct((b, n_exp, cp, d), BF16),
        compiler_params=_params("parallel", "parallel", "arbitrary"),
        name="expert_gather",
    )(lo_flat, nw_flat, h2, pos)


def _ffn_kernel(x_ref, wg_ref, wu_ref, wd_ref, o_ref, acc_ref, *, cap):
    f = pl.program_id(2)
    x = x_ref[0, 0]
    a = jnp.dot(x, wg_ref[0], preferred_element_type=F32)
    u = jnp.dot(x, wu_ref[0], preferred_element_type=F32)
    part = jnp.dot((a * jax.nn.sigmoid(a) * u).astype(BF16), wd_ref[0], preferred_element_type=F32)

    @pl.when(f == 0)
    def _():
        acc_ref[...] = part

    @pl.when(f > 0)
    def _():
        acc_ref[...] += part

    @pl.when(f == pl.num_programs(2) - 1)
    def _():
        o_ref[0, 0, 0:cap, :] = acc_ref[...].astype(o_ref.dtype)
        o_ref[0, 0, cap:, :] = jnp.zeros((o_ref.shape[2] - cap, o_ref.shape[3]), o_ref.dtype)


def expert_ffn(xe, w_gate, w_up, w_down, cap):
    b, n_exp, cp, d = xe.shape
    ff = w_gate.shape[2]
    tf = _pick_tile(ff, 512, LANES)
    return pl.pallas_call(
        functools.partial(_ffn_kernel, cap=cap),
        grid=(n_exp, b, ff // tf),
        in_specs=[pl.BlockSpec((1, 1, cap, d), lambda e, i, f: (i, e, 0, 0)),
                  pl.BlockSpec((1, d, tf), lambda e, i, f: (e, 0, f)),
                  pl.BlockSpec((1, d, tf), lambda e, i, f: (e, 0, f)),
                  pl.BlockSpec((1, tf, d), lambda e, i, f: (e, f, 0))],
        out_specs=pl.BlockSpec((1, 1, cp, d), lambda e, i, f: (i, e, 0, 0)),
        out_shape=jax.ShapeDtypeStruct((b, n_exp, cp, d), BF16),
        scratch_shapes=[pltpu.VMEM((cap, d), F32)],
        compiler_params=_params("parallel", "parallel", "arbitrary"),
        name="expert_ffn",
    )(xe, w_gate, w_up, w_down)


def _combine_kernel(lo_ref, nw_ref, y_hbm, pos_ref, g_ref, x_ref, ga_ref, o_ref, ybuf, yov, acc_ref, sem, osem,
                    *, nt, n_exp):
    i, t = pl.program_id(0), pl.program_id(1)
    tm = x_ref.shape[1]
    step = i * nt + t
    cur = step % 2

    def first_windows(s, buf):
        si, st = s // nt, s % nt
        copies = []
        for e in range(n_exp):
            lo = pl.multiple_of(lo_ref[(si * n_exp + e) * nt + st], BF16_SUBLANES)
            copies.append(pltpu.make_async_copy(y_hbm.at[si, e, pl.ds(lo, SEL_WIN), :],
                                                ybuf.at[buf, pl.ds(e * SEL_WIN, SEL_WIN), :], sem.at[buf, e]))
        return copies

    @pl.when(step == 0)
    def _():
        for copy in first_windows(step, cur):
            copy.start()

    @pl.when(step + 1 < pl.num_programs(0) * nt)
    def _():
        for copy in first_windows(step + 1, 1 - cur):
            copy.start()

    slot = lax.broadcasted_iota(jnp.int32, (SEL_WIN, tm), 0)

    def gated_onehot(e, base):
        return jnp.where(slot == pos_ref[0, e] - base, g_ref[0, e], 0.0).astype(BF16)

    dn = (((0,), (0,)), ((), ()))
    sel = jnp.concatenate([gated_onehot(e, lo_ref[(i * n_exp + e) * nt + t]) for e in range(n_exp)], axis=0)
    for copy in first_windows(step, cur):
        copy.wait()
    acc_ref[...] = lax.dot_general(sel, ybuf[cur], dn, preferred_element_type=F32)

    for e in range(n_exp):
        idx = (i * n_exp + e) * nt + t
        lo = lo_ref[idx]

        def extra(w, carry, e=e, lo=lo):
            base = pl.multiple_of(lo + w * SEL_WIN, BF16_SUBLANES)
            copy = pltpu.make_async_copy(y_hbm.at[i, e, pl.ds(base, SEL_WIN), :], yov, osem.at[0])
            copy.start()
            copy.wait()
            acc_ref[...] += lax.dot_general(gated_onehot(e, base), yov[...], dn, preferred_element_type=F32)
            return carry

        lax.fori_loop(1, nw_ref[idx], extra, 0)

    o_ref[0] = x_ref[0] + ga_ref[0] * acc_ref[...]


def expert_combine(y, pos, gates, lo_flat, nw_flat, x_new, ga2):
    b, s, d = x_new.shape
    n_exp = pos.shape[1]
    tm = SEL_TILE
    nt = s // tm
    sel = pl.BlockSpec((1, n_exp, 1, tm), lambda i, t, lo, nw: (i, 0, 0, t))
    return pl.pallas_call(
        functools.partial(_combine_kernel, nt=nt, n_exp=n_exp),
        grid_spec=pltpu.PrefetchScalarGridSpec(
            num_scalar_prefetch=2,
            grid=(b, nt),
            in_specs=[pl.BlockSpec(memory_space=pl.ANY), sel, sel,
                      pl.BlockSpec((1, tm, d), lambda i, t, lo, nw: (i, t, 0)),
                      pl.BlockSpec((1, 1, d), lambda i, t, lo, nw: (i, 0, 0))],
            out_specs=pl.BlockSpec((1, tm, d), lambda i, t, lo, nw: (i, t, 0)),
            scratch_shapes=[pltpu.VMEM((2, n_exp * SEL_WIN, d), BF16), pltpu.VMEM((SEL_WIN, d), BF16),
                            pltpu.VMEM((tm, d), F32), pltpu.SemaphoreType.DMA((2, n_exp)),
                            pltpu.SemaphoreType.DMA((1,))]),
        out_shape=jax.ShapeDtypeStruct((b, s, d), F32),
        compiler_params=_params("arbitrary", "arbitrary"),
        name="expert_combine",
    )(lo_flat, nw_flat, y, pos, gates, x_new, ga2.reshape(b, 1, d))


def expert_choice_moe(x_new, h2, aff, ga2, w_gate, w_up, w_down):
    b, s, d = x_new.shape
    n_exp = w_gate.shape[0]
    cap = CAPACITY_FACTOR * s // n_exp
    aff_e = aff[:, :, :n_exp].transpose(0, 2, 1)
    pos, off = expert_choice_select(aff_e, cap)
    start = off[:, :, ::SEL_TILE // LANES]
    count = jnp.diff(start, axis=-1, append=jnp.full((b, n_exp, 1), cap, start.dtype))
    lo = (start // BF16_SUBLANES) * BF16_SUBLANES
    nw = jnp.where(count > 0, (start - lo + count + SEL_WIN - 1) // SEL_WIN, 0)
    lo_flat, nw_flat = lo.reshape(-1), nw.reshape(-1)
    xe = expert_gather(h2, pos, lo_flat, nw_flat, cap)
    y = expert_ffn(xe, w_gate.astype(BF16), w_up.astype(BF16), w_down.astype(BF16), cap)
    return expert_combine(y, pos, aff_e.reshape(b, n_exp, 1, s), lo_flat, nw_flat, x_new, ga2)


def kernel(x, c, ctx, c_ctx, w_mod, b_mod, g_norm1, g_norm2, w_in, q_gain_a, k_gain_a, lam_q1, lam_k1, lam_q2,
           lam_k2, subln_gain, q_gain_b, k_gain_b, rel_pos_bias, w_branch_a, w_branch_b, w_out, w_router,
           w_exp_gate, w_exp_up, w_exp_down):
    assert w_mod.shape[0] == 1, "single-layer block"
    lam_init = 0.8 - 0.6 * math.exp(-0.3 * 0)
    lamv = jnp.stack([lam_q1[0], lam_k1[0], lam_q2[0], lam_k2[0]]).astype(F32)
    lat, proj, y_a, y_b, off_gate = _attention_branches(
        x, c, ctx, c_ctx, w_mod[0], b_mod[0], g_norm1[0], w_in[0], q_gain_a[0], k_gain_a[0], lamv, subln_gain[0],
        q_gain_b[0], k_gain_b[0], rel_pos_bias[0], lam_init)
    _, _, ga1, sh2, sc2, ga2 = lat
    x_new, h2, aff = merge_and_route(y_a, y_b, proj, off_gate, ctx.shape[1], x, w_branch_a[0], w_branch_b[0],
                                     w_out[0], ga1, g_norm2[0], sh2, sc2, w_router[0])
    return expert_choice_moe(x_new, h2, aff, ga2, w_exp_gate[0], w_exp_up[0], w_exp_down[0])
```

```python
import functools
import math

import jax
import jax.numpy as jnp
from jax import lax
from jax.experimental import pallas as pl
from jax.experimental.pallas import tpu as pltpu

GRID_W = 64
N_HEADS_A = 8
HEAD_DIM_A = 64
N_HEADS_B = 8
HEAD_DIM_B = 128
WIN_H = 8
WIN_W = 16
CAPACITY_FACTOR = 2
ROPE_BASE = 10000.0
EPS = 1e-6
NEG_INF = -1e30

LANES = 128
BF16_SUBLANES = 16
VMEM_LIMIT_BYTES = 56 * 1024 * 1024

F32 = jnp.float32
BF16 = jnp.bfloat16


def _params(*sem):
    return pltpu.CompilerParams(dimension_semantics=sem, vmem_limit_bytes=VMEM_LIMIT_BYTES)


def _pick_tile(n, cap, mult):
    best = None
    for t in range(mult, min(n, cap) + 1, mult):
        if n % t == 0:
            best = t
    assert best is not None, (n, cap, mult)
    return best


def _mod_kernel(cc_ref, w_ref, b_ref, o_ref):
    cc = cc_ref[...]
    act = cc * jax.nn.sigmoid(cc)
    o_ref[...] = jnp.dot(act, w_ref[...], preferred_element_type=F32,
                         precision=lax.Precision.HIGHEST) + b_ref[...]


def adaln_mod(cc, w_mod, b_mod):
    r, d = cc.shape
    n = w_mod.shape[1]
    tn = _pick_tile(n, 1024, LANES)
    return pl.pallas_call(
        _mod_kernel,
        grid=(n // tn,),
        in_specs=[pl.BlockSpec((r, d), lambda j: (0, 0)),
                  pl.BlockSpec((d, tn), lambda j: (0, j)),
                  pl.BlockSpec((1, tn), lambda j: (0, j))],
        out_specs=pl.BlockSpec((r, tn), lambda j: (0, j)),
        out_shape=jax.ShapeDtypeStruct((r, n), F32),
        compiler_params=_params("parallel"),
        name="adaln_mod",
    )(cc, w_mod, b_mod.reshape(1, n))


def _norm_mod_kernel(ctx_ref, x_ref, g_ref, sh_ref, sc_ref, csh_ref, csc_ref, o_ref):
    def emit(src_ref, shift, scale):
        xf = src_ref[0]
        ms = jnp.mean(xf * xf, axis=-1, keepdims=True)
        y = xf * lax.rsqrt(ms + EPS) * g_ref[...]
        o_ref[0] = (y * (1.0 + scale) + shift).astype(o_ref.dtype)

    t = pl.program_id(1)

    @pl.when(t == 0)
    def _():
        emit(ctx_ref, csh_ref[...], csc_ref[...])

    @pl.when(t > 0)
    def _():
        emit(x_ref, sh_ref[0], sc_ref[0])


def norm_modulate_tokens(ctx, x, g, sh, sc, csh, csc):
    b, s, d = x.shape
    l = ctx.shape[1]
    assert s % l == 0
    nt = (l + s) // l
    vec = lambda a: a.reshape(1, d)
    return pl.pallas_call(
        _norm_mod_kernel,
        grid=(b, nt),
        in_specs=[pl.BlockSpec((1, l, d), lambda i, t: (i, 0, 0)),
                  pl.BlockSpec((1, l, d), lambda i, t: (i, jnp.maximum(t - 1, 0), 0)),
                  pl.BlockSpec((1, d), lambda i, t: (0, 0)),
                  pl.BlockSpec((1, 1, d), lambda i, t: (i, 0, 0)),
                  pl.BlockSpec((1, 1, d), lambda i, t: (i, 0, 0)),
                  pl.BlockSpec((1, d), lambda i, t: (0, 0)),
                  pl.BlockSpec((1, d), lambda i, t: (0, 0))],
        out_specs=pl.BlockSpec((1, l, d), lambda i, t: (i, t, 0)),
        out_shape=jax.ShapeDtypeStruct((b, l + s, d), BF16),
        compiler_params=_params("parallel", "arbitrary"),
        name="norm_modulate",
    )(ctx, x, vec(g), sh.reshape(b, 1, d), sc.reshape(b, 1, d), vec(csh), vec(csc))


def _matmul_kernel(x_ref, w_ref, o_ref):
    o_ref[...] = jnp.dot(x_ref[...], w_ref[...], preferred_element_type=F32).astype(o_ref.dtype)


def matmul_bf16(x, w):
    m, k = x.shape
    n = w.shape[1]
    tm = _pick_tile(m, 1024, 8 * BF16_SUBLANES)
    tn = _pick_tile(n, 1024, LANES)
    return pl.pallas_call(
        _matmul_kernel,
        grid=(m // tm, n // tn),
        in_specs=[pl.BlockSpec((tm, k), lambda i, j: (i, 0)),
                  pl.BlockSpec((k, tn), lambda i, j: (0, j))],
        out_specs=pl.BlockSpec((tm, tn), lambda i, j: (i, j)),
        out_shape=jax.ShapeDtypeStruct((m, n), BF16),
        compiler_params=_params("parallel", "arbitrary"),
        name="in_proj",
    )(x, w)


def _dot_f32_by_01(x, mat_ref):
    hi = x.astype(BF16)
    lo = (x - hi.astype(F32)).astype(BF16)
    return (jnp.dot(hi, mat_ref[...], preferred_element_type=F32)
            + jnp.dot(lo, mat_ref[...], preferred_element_type=F32))


def _prep_kernel(*refs, n_heads, mode, rope, scale):
    if rope:
        x_ref, gain_ref, avg_ref, cos_ref, sin_ref, rot_ref, o_ref = refs
    elif mode != "copy":
        x_ref, gain_ref, avg_ref, o_ref = refs
    else:
        x_ref, o_ref = refs
    for h in range(n_heads):
        x = x_ref[0, :, h * LANES:(h + 1) * LANES]
        if mode == "copy":
            o_ref[0, h, :, 0:LANES] = x
            if o_ref.shape[3] > LANES:
                o_ref[0, h, :, LANES:] = jnp.ones((x.shape[0], o_ref.shape[3] - LANES), o_ref.dtype)
            continue
        x = x.astype(F32)
        x = x * lax.rsqrt(_dot_f32_by_01(x * x, avg_ref) + EPS) * gain_ref[...]
        if rope:
            x = x * cos_ref[...] + _dot_f32_by_01(x, rot_ref) * sin_ref[...]
        if scale != 1.0:
            x = x * scale
        o_ref[0, h] = x.astype(o_ref.dtype)


def prep_heads(proj3, col_off, n_heads, tok_off, n_tok, *, mode, gain=None, rope_tabs=None, scale=1.0,
               out_width=LANES):
    b, t_all, _ = proj3.shape
    width = n_heads * LANES
    assert col_off % width == 0
    tt = _pick_tile(math.gcd(n_tok, tok_off) if tok_off else n_tok, 512, 8 * BF16_SUBLANES)
    assert tok_off % tt == 0 and n_tok % tt == 0
    cb, tb = col_off // width, tok_off // tt
    in_specs = [pl.BlockSpec((1, tt, width), lambda ti, i: (i, ti + tb, cb))]
    args = [proj3]
    square = pl.BlockSpec((LANES, LANES), lambda ti, i: (0, 0))
    lane = jnp.arange(LANES)
    if mode != "copy":
        group = HEAD_DIM_A if mode == "pair" else LANES
        avg = jnp.where(lane[:, None] // group == lane[None, :] // group, 1.0 / group, 0.0).astype(BF16)
        in_specs += [pl.BlockSpec((1, LANES), lambda ti, i: (0, 0)), square]
        args += [gain.reshape(1, LANES).astype(F32), avg]
    if rope_tabs is not None:
        quarter = HEAD_DIM_A // 4
        src, dst = lane[:, None], lane[None, :]
        first = dst % (2 * quarter) < quarter
        rot = jnp.where(first & (src == dst + quarter), -1.0,
                        jnp.where(~first & (src == dst - quarter), 1.0, 0.0)).astype(BF16)
        for tab in rope_tabs:
            in_specs.append(pl.BlockSpec((tt, LANES), lambda ti, i: (ti, 0)))
            args.append(tab)
        in_specs.append(square)
        args.append(rot)
    return pl.pallas_call(
        functools.partial(_prep_kernel, n_heads=n_heads, mode=mode, rope=rope_tabs is not None, scale=scale),
        grid=(n_tok // tt, b),
        in_specs=in_specs,
        out_specs=pl.BlockSpec((1, n_heads, tt, out_width), lambda ti, i: (i, 0, ti, 0)),
        out_shape=jax.ShapeDtypeStruct((b, n_heads, n_tok, out_width), BF16),
        compiler_params=_params("parallel", "arbitrary"),
        name="prep_" + mode,
    )(*args)


def rope_tables(n_lat, n_ctx):
    t = jnp.arange(n_lat, dtype=jnp.int32)
    row = (t // GRID_W).astype(F32)
    col = (t % GRID_W).astype(F32)
    half = HEAD_DIM_A // 2
    inv_freq = ROPE_BASE ** (-jnp.arange(0, half, 2, dtype=F32) / half)

    def tab(pos):
        ang = pos[:, None] * inv_freq[None, :]
        ang = jnp.concatenate([ang, ang], axis=-1)
        return jnp.cos(ang), jnp.sin(ang)

    cr, sr = tab(row)
    cc, sc = tab(col)
    cos = jnp.tile(jnp.concatenate([cr, cc], axis=-1), (1, LANES // HEAD_DIM_A))
    sin = jnp.tile(jnp.concatenate([sr, sc], axis=-1), (1, LANES // HEAD_DIM_A))
    pad = lambda a, v: jnp.concatenate([jnp.full((n_ctx, LANES), v, F32), a], axis=0)
    return pad(cos, 1.0), pad(sin, 0.0)


def _diff_attn_kernel(lamv_ref, q_ref, k_ref, v_ref, sg_ref, o_ref, m_s, a_s, qq_s, sa, sb, *, tk, lam_init):
    n_k = k_ref.shape[2] // tk
    tq = q_ref.shape[2]
    q = q_ref[0, 0]
    lo = lax.broadcasted_iota(jnp.int32, q.shape, 1) < HEAD_DIM_A
    zero = jnp.zeros_like(q)
    qq_s[0:tq, :] = jnp.where(lo, q, zero)
    qq_s[tq:, :] = jnp.where(lo, zero, q)
    m_s[...] = jnp.full(m_s.shape, -jnp.inf, F32)
    a_s[...] = jnp.zeros(a_s.shape, F32)

    def produce(c, s_ref):
        k = k_ref[0, 0, pl.ds(pl.multiple_of(c * tk, tk), tk), :]
        s_ref[...] = lax.dot_general(qq_s[...], k, (((1,), (1,)), ((), ())), preferred_element_type=F32)

    def consume(c, s_ref):
        v = v_ref[0, 0, pl.ds(pl.multiple_of(c * tk, tk), tk), :]
        s = s_ref[...]
        m_old = m_s[...]
        m_new = jnp.maximum(m_old, jnp.max(s, axis=-1, keepdims=True))
        alpha = jnp.exp2(m_old - m_new)
        p = jnp.exp2((s - jnp.tile(m_new, (1, tk // LANES))).astype(BF16))
        a_s[...] = jnp.tile(alpha, (1, 2)) * a_s[...] + jnp.dot(p, v, preferred_element_type=F32)
        m_s[...] = m_new

    bufs = (sa, sb)
    produce(0, sa)
    for c in range(n_k):
        if c + 1 < n_k:
            produce(c + 1, bufs[(c + 1) % 2])
        consume(c, bufs[c % 2])

    lv = lamv_ref[...]
    lam = (jnp.exp(jnp.sum(lv[0:1] * lv[1:2], axis=-1, keepdims=True))
           - jnp.exp(jnp.sum(lv[2:3] * lv[3:4], axis=-1, keepdims=True)) + lam_init)
    y = (a_s[0:tq, 0:LANES] / a_s[0:tq, LANES:] - lam * (a_s[tq:, 0:LANES] / a_s[tq:, LANES:]))
    y = y * lax.rsqrt(jnp.mean(y * y, axis=-1, keepdims=True) + EPS) * sg_ref[...]
    o_ref[0] = (y * (1.0 - lam_init)).astype(o_ref.dtype)


DIFF_ATTN_TQ = 512
DIFF_ATTN_TK = 768


def diff_attention(lamv, q, k, v1, subln_gain, lam_init):
    b, h, s, _ = q.shape
    t = k.shape[2]
    tq = _pick_tile(s, DIFF_ATTN_TQ, 8 * BF16_SUBLANES)
    tk = _pick_tile(t, DIFF_ATTN_TK, LANES)
    stat = lambda w: pltpu.VMEM((2 * tq, w), F32)
    return pl.pallas_call(
        functools.partial(_diff_attn_kernel, tk=tk, lam_init=lam_init),
        grid=(b, h, s // tq),
        in_specs=[pl.BlockSpec((4, HEAD_DIM_A), lambda i, j, qi: (0, 0)),
                  pl.BlockSpec((1, 1, tq, LANES), lambda i, j, qi: (i, j, qi, 0)),
                  pl.BlockSpec((1, 1, t, LANES), lambda i, j, qi: (i, j, 0, 0)),
                  pl.BlockSpec((1, 1, t, 2 * LANES), lambda i, j, qi: (i, j, 0, 0)),
                  pl.BlockSpec((1, LANES), lambda i, j, qi: (0, 0))],
        out_specs=pl.BlockSpec((1, tq, LANES), lambda i, j, qi: (i, qi, j)),
        out_shape=jax.ShapeDtypeStruct((b, s, h * LANES), BF16),
        scratch_shapes=[stat(LANES), stat(2 * LANES), pltpu.VMEM((2 * tq, LANES), BF16), stat(tk), stat(tk)],
        compiler_params=_params("parallel", "parallel", "arbitrary"),
        name="diff_attention",
    )(lamv, q, k, v1, subln_gain.reshape(1, LANES).astype(F32))


NA_QROWS = 8
NA_KROWS = NA_QROWS + WIN_H
ROWS_PER_VREG = LANES // GRID_W


def _na_kernel(q_ref, k_ref, v_ref, bias_ref, o_ref, sa_ref, sb_ref, *, n_ctx, rows):
    nq, nkw = NA_QROWS * GRID_W, NA_KROWS * GRID_W
    masked = bias_ref.shape[2] - 1
    k_ctx = k_ref[0, 0, 0:n_ctx, :]
    v_ctx = v_ref[0, 0, 0:n_ctx, :]
    dn = (((1,), (1,)), ((), ()))

    def block(j, s_ref):
        r0 = j * NA_QROWS
        w0 = jnp.clip(r0 - WIN_H // 2, 0, rows - NA_KROWS)
        q_off = pl.multiple_of(r0 * GRID_W, nq)
        k_off = pl.multiple_of(n_ctx + w0 * GRID_W, GRID_W)
        q = q_ref[0, 0, pl.ds(q_off, nq), :]
        s_ref[:, 0:n_ctx] = lax.dot_general(q, k_ctx, dn, preferred_element_type=F32)
        s_ref[:, n_ctx:] = lax.dot_general(q, k_ref[0, 0, pl.ds(k_off, nkw), :], dn, preferred_element_type=F32)
        for i in range(NA_QROWS):
            r = r0 + i
            below = r - jnp.clip(r - WIN_H // 2, 0, rows - WIN_H)
            for p in range(NA_KROWS // ROWS_PER_VREG):
                a0 = w0 + p * ROWS_PER_VREG - r + (WIN_H - 1)
                touches = (a0 >= WIN_H - ROWS_PER_VREG - below) & (a0 < 2 * WIN_H - 1 - below)
                entry = jnp.where(touches, a0 + ROWS_PER_VREG - 1, masked)
                cols = slice(n_ctx + p * LANES, n_ctx + (p + 1) * LANES)
                s_ref[i * GRID_W:(i + 1) * GRID_W, cols] += bias_ref[0, below, entry]
        s = s_ref[...]
        p = jnp.exp2(s - jnp.max(s, axis=-1, keepdims=True)).astype(BF16)
        acc = (jnp.dot(p[:, 0:n_ctx], v_ctx, preferred_element_type=F32)
               + jnp.dot(p[:, n_ctx:], v_ref[0, 0, pl.ds(k_off, nkw), :], preferred_element_type=F32))
        o_ref[0, pl.ds(q_off, nq), :] = (acc[:, 0:LANES] / acc[:, LANES:]).astype(o_ref.dtype)

    def body(jj, carry):
        block(2 * jj, sa_ref)
        block(2 * jj + 1, sb_ref)
        return carry

    lax.fori_loop(0, rows // (2 * NA_QROWS), body, 0)


def na_bias_table(rpb):
    n_rel = 2 * WIN_H - 1
    col = jnp.arange(GRID_W, dtype=jnp.int32)
    col_start = jnp.clip(col - WIN_W // 2, 0, GRID_W - WIN_W)
    in_win = (col[None, :] >= col_start[:, None]) & (col[None, :] < col_start[:, None] + WIN_W)
    idx_c = jnp.clip(col[None, :] - col[:, None] + WIN_W - 1, 0, 2 * WIN_W - 2)
    hp = lax.Precision.HIGHEST
    pick_c = (idx_c[:, :, None] == jnp.arange(rpb.shape[2])[None, None, :]).astype(F32)
    toeplitz = jnp.einsum("hac,qkc->haqk", rpb.astype(F32), pick_c, precision=hp)
    below = jnp.arange(WIN_H)[:, None, None]
    entry = jnp.arange(n_rel + ROWS_PER_VREG - 1)[None, :, None]
    rel = entry - (ROWS_PER_VREG - 1) + (jnp.arange(LANES) // GRID_W)[None, None, :]
    valid = (rel >= WIN_H - 1 - below) & (rel < n_rel - below)
    pick_r = ((rel[..., None] == jnp.arange(n_rel)) & valid[..., None]).astype(F32)
    tiles = jnp.einsum("vela,haql->hveql", pick_r, jnp.tile(toeplitz, (1, 1, 1, ROWS_PER_VREG)), precision=hp)
    keep = valid[None, :, :, None, :] & jnp.tile(in_win, (1, ROWS_PER_VREG))[None, None, None]
    tiles = jnp.where(keep, tiles * math.log2(math.e), NEG_INF)
    return jnp.concatenate([tiles, jnp.full_like(tiles[:, :, :1], NEG_INF)], axis=2)


def neighbourhood_attention(q, k, v1, bias, n_ctx):
    b, h, s, _ = q.shape
    t = k.shape[2]
    rows = s // GRID_W
    assert rows % (2 * NA_QROWS) == 0 and rows >= NA_KROWS and LANES % GRID_W == 0
    nq = NA_QROWS * GRID_W
    return pl.pallas_call(
        functools.partial(_na_kernel, n_ctx=n_ctx, rows=rows),
        grid=(b, h),
        in_specs=[pl.BlockSpec((1, 1, s, LANES), lambda i, j: (i, j, 0, 0)),
                  pl.BlockSpec((1, 1, t, LANES), lambda i, j: (i, j, 0, 0)),
                  pl.BlockSpec((1, 1, t, 2 * LANES), lambda i, j: (i, j, 0, 0)),
                  pl.BlockSpec((1,) + bias.shape[1:], lambda i, j: (j, 0, 0, 0, 0))],
        out_specs=pl.BlockSpec((1, s, LANES), lambda i, j: (i, 0, j)),
        out_shape=jax.ShapeDtypeStruct((b, s, h * LANES), BF16),
        scratch_shapes=[pltpu.VMEM((nq, n_ctx + NA_KROWS * GRID_W), F32)] * 2,
        compiler_params=_params("parallel", "parallel"),
        name="neighbourhood_attention",
    )(q, k, v1, bias)


def _attention_branches(x, c, ctx, c_ctx, w_mod, b_mod, g_norm1, w_in, q_gain_a, k_gain_a, lamv, subln_gain,
                        q_gain_b, k_gain_b, rpb, lam_init):
    b, s, d = x.shape
    l = ctx.shape[1]
    t = l + s
    wa, wb = N_HEADS_A * LANES, N_HEADS_B * LANES
    off_qa, off_qb, off_gate = 0, wa, wa + wb
    off_ka = off_gate + 2 * d
    off_va, off_kb, off_vb = off_ka + wa, off_ka + 2 * wa, off_ka + 2 * wa + wb

    n_rows = -(-(b + 1) // 8) * 8
    cc = jnp.zeros((n_rows, d), F32).at[:b].set(c).at[b].set(c_ctx)
    mod = adaln_mod(cc, w_mod, b_mod)
    lat = [mod[:b, i * d:(i + 1) * d] for i in range(6)]
    cmod = [mod[b, i * d:(i + 1) * d] for i in range(6)]

    h = norm_modulate_tokens(ctx, x, g_norm1, lat[0], lat[1], cmod[0], cmod[1])
    proj = matmul_bf16(h.reshape(b * t, d), w_in.astype(BF16)).reshape(b, t, w_in.shape[1])

    tabs = rope_tables(s, l)
    tabs_q = tuple(a[l:] for a in tabs)
    two = lambda g: jnp.tile(g, LANES // HEAD_DIM_A)
    qa = prep_heads(proj, off_qa, N_HEADS_A, l, s, mode="pair", gain=two(q_gain_a), rope_tabs=tabs_q,
                    scale=HEAD_DIM_A ** -0.5 * math.log2(math.e))
    ka = prep_heads(proj, off_ka, N_HEADS_A, 0, t, mode="pair", gain=two(k_gain_a), rope_tabs=tabs)
    va = prep_heads(proj, off_va, N_HEADS_A, 0, t, mode="copy", out_width=2 * LANES)
    qb = prep_heads(proj, off_qb, N_HEADS_B, l, s, mode="full", gain=q_gain_b,
                    scale=HEAD_DIM_B ** -0.5 * math.log2(math.e))
    kb = prep_heads(proj, off_kb, N_HEADS_B, 0, t, mode="full", gain=k_gain_b)
    vb = prep_heads(proj, off_vb, N_HEADS_B, 0, t, mode="copy", out_width=2 * LANES)

    y_a = diff_attention(lamv, qa, ka, va, subln_gain, lam_init)
    y_b = neighbourhood_attention(qb, kb, vb, na_bias_table(rpb), l)
    return lat, proj, y_a, y_b, off_gate


def _merge_kernel(ya_ref, yb_ref, ga_ref, gb_ref, x_ref, wa_ref, wb_ref, wo_ref, g1_ref, gn_ref, sh_ref, sc_ref,
                  wrh_ref, wrl_ref, xo_ref, h2_ref, aff_ref, *, n_exp):
    ta = jnp.dot(ya_ref[0], wa_ref[...], preferred_element_type=F32)
    tb = jnp.dot(yb_ref[0], wb_ref[...], preferred_element_type=F32)
    u = jax.nn.sigmoid(ga_ref[0].astype(F32)) * ta + jax.nn.sigmoid(gb_ref[0].astype(F32)) * tb
    mix = jnp.dot(u.astype(BF16), wo_ref[...], preferred_element_type=F32)
    xn = x_ref[0] + g1_ref[0] * mix
    xo_ref[0] = xn
    ms = jnp.mean(xn * xn, axis=-1, keepdims=True)
    h2 = (xn * lax.rsqrt(ms + EPS) * gn_ref[...]) * (1.0 + sc_ref[0]) + sh_ref[0]
    hi = h2.astype(BF16)
    h2_ref[0] = hi
    lo = (h2 - hi.astype(F32)).astype(BF16)
    logits = (jnp.dot(hi, wrh_ref[...], preferred_element_type=F32)
              + jnp.dot(lo, wrh_ref[...], preferred_element_type=F32)
              + jnp.dot(hi, wrl_ref[...], preferred_element_type=F32))
    valid = lax.broadcasted_iota(jnp.int32, logits.shape, 1) < n_exp
    logits = jnp.where(valid, logits, -jnp.inf)
    p = jnp.exp(logits - jnp.max(logits, axis=-1, keepdims=True))
    aff_ref[0] = p / jnp.sum(p, axis=-1, keepdims=True)


def merge_and_route(y_a, y_b, proj3, off_gate, n_ctx, x, w_a, w_b, w_o, ga1, g_norm2, sh2, sc2, w_router):
    b, s, d = x.shape
    n_exp = w_router.shape[1]
    tm = _pick_tile(math.gcd(s, n_ctx), 256, 8 * BF16_SUBLANES)
    assert off_gate % d == 0 and n_exp <= LANES
    tb, gb = n_ctx // tm, off_gate // d
    wr = jnp.zeros((d, LANES), F32).at[:, :n_exp].set(w_router)
    wr_hi = wr.astype(BF16)
    wr_lo = (wr - wr_hi.astype(F32)).astype(BF16)
    const = lambda shape: pl.BlockSpec(shape, lambda i, t: (0,) * len(shape), pipeline_mode=pl.Buffered(1))
    tok = lambda w: pl.BlockSpec((1, tm, w), lambda i, t: (i, t, 0))
    per_b = pl.BlockSpec((1, 1, d), lambda i, t: (i, 0, 0))
    return pl.pallas_call(
        functools.partial(_merge_kernel, n_exp=n_exp),
        grid=(b, s // tm),
        in_specs=[tok(y_a.shape[2]), tok(y_b.shape[2]),
                  pl.BlockSpec((1, tm, d), lambda i, t: (i, t + tb, gb)),
                  pl.BlockSpec((1, tm, d), lambda i, t: (i, t + tb, gb + 1)),
                  tok(d), const(w_a.shape), const(w_b.shape), const(w_o.shape),
                  per_b, const((1, d)), per_b, per_b, const((d, LANES)), const((d, LANES))],
        out_specs=[tok(d), tok(d), tok(LANES)],
        out_shape=[jax.ShapeDtypeStruct((b, s, d), F32), jax.ShapeDtypeStruct((b, s, d), BF16),
                   jax.ShapeDtypeStruct((b, s, LANES), F32)],
        compiler_params=_params("parallel", "arbitrary"),
        name="merge_route",
    )(y_a, y_b, proj3, proj3, x, w_a.astype(BF16), w_b.astype(BF16), w_o.astype(BF16),
      ga1.reshape(b, 1, d), g_norm2.reshape(1, d), sh2.reshape(b, 1, d), sc2.reshape(b, 1, d), wr_hi, wr_lo)


def _prefix_incl(mask, tri_ref, ones_ref, blk_ref):
    e, nc, _ = mask.shape
    m2 = mask.reshape(e * nc, LANES).astype(BF16)
    within = jnp.dot(m2, tri_ref[...], preferred_element_type=F32)
    row_tot = jnp.dot(m2, ones_ref[...], preferred_element_type=F32)
    before = jnp.dot(blk_ref[...], row_tot.astype(BF16), preferred_element_type=F32)
    return (within + before).reshape(e, nc, LANES), before.reshape(e, nc, LANES)


def _topk_kernel(aff_ref, tri_ref, ones_ref, blk_ref, pos_ref, off_ref, *, cap):
    a = aff_ref[0]
    bits = pltpu.bitcast(a, jnp.int32)

    def count(mask):
        c = jnp.sum(jnp.where(mask, 1.0, 0.0), axis=2, keepdims=True)
        return jnp.sum(c, axis=1, keepdims=True)

    def step(i, cur):
        cand = cur | jnp.left_shift(jnp.int32(1), 30 - i)
        return jnp.where(count(bits >= cand) >= cap, cand, cur)

    thr = lax.fori_loop(0, 31, step, jnp.zeros((a.shape[0], 1, 1), jnp.int32))
    gt = bits > thr
    eq = bits == thr
    need = cap - count(gt)
    eq_f = jnp.where(eq, 1.0, 0.0)
    eq_incl, _ = _prefix_incl(eq_f, tri_ref, ones_ref, blk_ref)
    sel = gt | (eq & ((eq_incl - eq_f) < need))
    sel_f = jnp.where(sel, 1.0, 0.0)
    sel_incl, before = _prefix_incl(sel_f, tri_ref, ones_ref, blk_ref)
    pos_ref[0] = jnp.where(sel, sel_incl - sel_f, -1.0).astype(jnp.int32)
    off_ref[0] = before.astype(jnp.int32)


def expert_choice_select(aff_e, cap):
    b, e, s = aff_e.shape
    nc = s // LANES
    idx = jnp.arange(LANES)
    tri = (idx[:, None] <= idx[None, :]).astype(BF16)
    ones = jnp.ones((LANES, LANES), BF16)
    r = jnp.arange(e * nc)
    blk = ((r[:, None] // nc == r[None, :] // nc) & (r[None, :] % nc < r[:, None] % nc)).astype(BF16)
    blk4 = pl.BlockSpec((1, e, nc, LANES), lambda i: (i, 0, 0, 0))
    const = lambda n: pl.BlockSpec((n, n), lambda i: (0, 0))
    pos, off = pl.pallas_call(
        functools.partial(_topk_kernel, cap=cap),
        grid=(b,),
        in_specs=[blk4, const(LANES), const(LANES), const(e * nc)],
        out_specs=[blk4, blk4],
        out_shape=[jax.ShapeDtypeStruct((b, e, nc, LANES), jnp.int32)] * 2,
        compiler_params=_params("parallel"),
        name="expert_choice_select",
    )(aff_e.reshape(b, e, nc, LANES), tri, ones, blk)
    return pos.reshape(b, e, 1, s), off[..., 0]


SEL_TILE = 2 * LANES
SEL_WIN = 64


def _gather_kernel(lo_ref, nw_ref, h_ref, pos_ref, o_ref, *, eg, nt, n_exp):
    i, g, t = pl.program_id(0), pl.program_id(1), pl.program_id(2)

    @pl.when(t == 0)
    def _():
        o_ref[...] = jnp.zeros(o_ref.shape, o_ref.dtype)

    h = h_ref[0]
    slot = lax.broadcasted_iota(jnp.int32, (SEL_WIN, h.shape[0]), 0)

    def onehot(k, base):
        return jnp.where(slot == pos_ref[0, k] - base, 1.0, 0.0).astype(BF16)

    def add_rows(k, base, rows):
        cur = o_ref[0, k, pl.ds(base, SEL_WIN), :]
        o_ref[0, k, pl.ds(base, SEL_WIN), :] = (cur.astype(F32) + rows).astype(o_ref.dtype)

    idx = [(i * n_exp + g * eg + k) * nt + t for k in range(eg)]
    lo = [pl.multiple_of(lo_ref[idx[k]], BF16_SUBLANES) for k in range(eg)]
    first = jnp.dot(jnp.concatenate([onehot(k, lo[k]) for k in range(eg)], axis=0), h,
                    preferred_element_type=F32)
    for k in range(eg):
        add_rows(k, lo[k], first[k * SEL_WIN:(k + 1) * SEL_WIN])

        def extra(w, carry, k=k):
            base = pl.multiple_of(lo[k] + w * SEL_WIN, BF16_SUBLANES)
            add_rows(k, base, jnp.dot(onehot(k, base), h, preferred_element_type=F32))
            return carry

        lax.fori_loop(1, nw_ref[idx[k]], extra, 0)


def expert_gather(h2, pos, lo_flat, nw_flat, cap, eg=4):
    b, s, d = h2.shape
    n_exp = pos.shape[1]
    tm = SEL_TILE
    nt = s // tm
    cp = cap + SEL_WIN
    return pl.pallas_call(
        functools.partial(_gather_kernel, eg=eg, nt=nt, n_exp=n_exp),
        grid_spec=pltpu.PrefetchScalarGridSpec(
            num_scalar_prefetch=2,
            grid=(b, n_exp // eg, nt),
            in_specs=[pl.BlockSpec((1, tm, d), lambda i, g, t, lo, nw: (i, t, 0)),
                      pl.BlockSpec((1, eg, 1, tm), lambda i, g, t, lo, nw: (i, g, 0, t))],
            out_specs=pl.BlockSpec((1, eg, cp, d), lambda i, g, t, lo, nw: (i, g, 0, 0))),
        out_shape=jax.ShapeDtypeStruct((b, n_exp, cp, d), BF16),
        compiler_params=_params("parallel", "parallel", "arbitrary"),
        name="expert_gather",
    )(lo_flat, nw_flat, h2, pos)


def _ffn_kernel(x_ref, wg_ref, wu_ref, wd_ref, o_ref, acc_ref, *, cap):
    f = pl.program_id(2)
    @pl.when(f == 0)
    def _():
        acc_ref[...] = jnp.zeros(acc_ref.shape, F32)

    x = x_ref[0, 0]
    a = jnp.dot(x, wg_ref[0].astype(BF16), preferred_element_type=F32)
    u = jnp.dot(x, wu_ref[0].astype(BF16), preferred_element_type=F32)
    acc_ref[...] += jnp.dot((a * jax.nn.sigmoid(a) * u).astype(BF16), wd_ref[0].astype(BF16),
                            preferred_element_type=F32)

    @pl.when(f == pl.num_programs(2) - 1)
    def _():
        o_ref[0, 0, 0:cap, :] = acc_ref[...].astype(o_ref.dtype)
        o_ref[0, 0, cap:, :] = jnp.zeros((o_ref.shape[2] - cap, o_ref.shape[3]), o_ref.dtype)


FFN_TF = 256


def expert_ffn(xe, w_gate, w_up, w_down, cap):
    b, n_exp, cp, d = xe.shape
    ff = w_gate.shape[2]
    tf = _pick_tile(ff, FFN_TF, LANES)
    return pl.pallas_call(
        functools.partial(_ffn_kernel, cap=cap),
        grid=(n_exp, b, ff // tf),
        in_specs=[pl.BlockSpec((1, 1, cap, d), lambda e, i, f: (i, e, 0, 0)),
                  pl.BlockSpec((1, d, tf), lambda e, i, f: (e, 0, f)),
                  pl.BlockSpec((1, d, tf), lambda e, i, f: (e, 0, f)),
                  pl.BlockSpec((1, tf, d), lambda e, i, f: (e, f, 0))],
        out_specs=pl.BlockSpec((1, 1, cp, d), lambda e, i, f: (i, e, 0, 0)),
        out_shape=jax.ShapeDtypeStruct((b, n_exp, cp, d), BF16),
        scratch_shapes=[pltpu.VMEM((cap, d), F32)],
        compiler_params=_params("parallel", "parallel", "arbitrary"),
        name="expert_ffn",
    )(xe, w_gate, w_up, w_down)


def _combine_kernel(lo_ref, nw_ref, y_hbm, pos_ref, g_ref, x_ref, ga_ref, o_ref, ybuf, yov, acc_ref, sem, osem,
                    *, nt, n_exp):
    i, t = pl.program_id(0), pl.program_id(1)
    tm = x_ref.shape[1]
    step = i * nt + t
    cur = step % 2

    def first_windows(s, buf):
        si, st = s // nt, s % nt
        copies = []
        for e in range(n_exp):
            lo = pl.multiple_of(lo_ref[(si * n_exp + e) * nt + st], BF16_SUBLANES)
            copies.append(pltpu.make_async_copy(y_hbm.at[si, e, pl.ds(lo, SEL_WIN), :],
                                                ybuf.at[buf, pl.ds(e * SEL_WIN, SEL_WIN), :], sem.at[buf, e]))
        return copies

    @pl.when(step == 0)
    def _():
        for copy in first_windows(step, cur):
            copy.start()

    @pl.when(step + 1 < pl.num_programs(0) * nt)
    def _():
        for copy in first_windows(step + 1, 1 - cur):
            copy.start()

    slot = lax.broadcasted_iota(jnp.int32, (SEL_WIN, tm), 0)

    def gated_onehot(e, base):
        return jnp.where(slot == pos_ref[0, e] - base, g_ref[0, e], 0.0).astype(BF16)

    dn = (((0,), (0,)), ((), ()))
    sel = jnp.concatenate([gated_onehot(e, lo_ref[(i * n_exp + e) * nt + t]) for e in range(n_exp)], axis=0)
    for copy in first_windows(step, cur):
        copy.wait()
    acc_ref[...] = lax.dot_general(sel, ybuf[cur], dn, preferred_element_type=F32)

    for e in range(n_exp):
        idx = (i * n_exp + e) * nt + t
        lo = lo_ref[idx]

        def extra(w, carry, e=e, lo=lo):
            base = pl.multiple_of(lo + w * SEL_WIN, BF16_SUBLANES)
            copy = pltpu.make_async_copy(y_hbm.at[i, e, pl.ds(base, SEL_WIN), :], yov, osem.at[0])
            copy.start()
            copy.wait()
            acc_ref[...] += lax.dot_general(gated_onehot(e, base), yov[...], dn, preferred_element_type=F32)
            return carry

        lax.fori_loop(1, nw_ref[idx], extra, 0)

    o_ref[0] = x_ref[0] + ga_ref[0] * acc_ref[...]


def expert_combine(y, pos, gates, lo_flat, nw_flat, x_new, ga2):
    b, s, d = x_new.shape
    n_exp = pos.shape[1]
    tm = SEL_TILE
    nt = s // tm
    sel = pl.BlockSpec((1, n_exp, 1, tm), lambda i, t, lo, nw: (i, 0, 0, t))
    return pl.pallas_call(
        functools.partial(_combine_kernel, nt=nt, n_exp=n_exp),
        grid_spec=pltpu.PrefetchScalarGridSpec(
            num_scalar_prefetch=2,
            grid=(b, nt),
            in_specs=[pl.BlockSpec(memory_space=pl.ANY), sel, sel,
                      pl.BlockSpec((1, tm, d), lambda i, t, lo, nw: (i, t, 0)),
                      pl.BlockSpec((1, 1, d), lambda i, t, lo, nw: (i, 0, 0))],
            out_specs=pl.BlockSpec((1, tm, d), lambda i, t, lo, nw: (i, t, 0)),
            scratch_shapes=[pltpu.VMEM((2, n_exp * SEL_WIN, d), BF16), pltpu.VMEM((SEL_WIN, d), BF16),
                            pltpu.VMEM((tm, d), F32), pltpu.SemaphoreType.DMA((2, n_exp)),
                            pltpu.SemaphoreType.DMA((1,))]),
        out_shape=jax.ShapeDtypeStruct((b, s, d), F32),
        compiler_params=_params("arbitrary", "arbitrary"),
        name="expert_combine",
    )(lo_flat, nw_flat, y, pos, gates, x_new, ga2.reshape(b, 1, d))


def expert_choice_moe(x_new, h2, aff, ga2, w_gate, w_up, w_down):
    b, s, d = x_new.shape
    n_exp = w_gate.shape[0]
    cap = CAPACITY_FACTOR * s // n_exp
    aff_e = aff[:, :, :n_exp].transpose(0, 2, 1)
    pos, off = expert_choice_select(aff_e, cap)
    start = off[:, :, ::SEL_TILE // LANES]
    count = jnp.diff(start, axis=-1, append=jnp.full((b, n_exp, 1), cap, start.dtype))
    lo = (start // BF16_SUBLANES) * BF16_SUBLANES
    nw = jnp.where(count > 0, (start - lo + count + SEL_WIN - 1) // SEL_WIN, 0)
    lo_flat, nw_flat = lo.reshape(-1), nw.reshape(-1)
    xe = expert_gather(h2, pos, lo_flat, nw_flat, cap)
    y = expert_ffn(xe, w_gate, w_up, w_down, cap)
    return expert_combine(y, pos, aff_e.reshape(b, n_exp, 1, s), lo_flat, nw_flat, x_new, ga2)


def kernel(x, c, ctx, c_ctx, w_mod, b_mod, g_norm1, g_norm2, w_in, q_gain_a, k_gain_a, lam_q1, lam_k1, lam_q2,
           lam_k2, subln_gain, q_gain_b, k_gain_b, rel_pos_bias, w_branch_a, w_branch_b, w_out, w_router,
           w_exp_gate, w_exp_up, w_exp_down):
    assert w_mod.shape[0] == 1, "single-layer block"
    lam_init = 0.8 - 0.6 * math.exp(-0.3 * 0)
    lamv = jnp.stack([lam_q1[0], lam_k1[0], lam_q2[0], lam_k2[0]]).astype(F32)
    lat, proj, y_a, y_b, off_gate = _attention_branches(
        x, c, ctx, c_ctx, w_mod[0], b_mod[0], g_norm1[0], w_in[0], q_gain_a[0], k_gain_a[0], lamv, subln_gain[0],
        q_gain_b[0], k_gain_b[0], rel_pos_bias[0], lam_init)
    _, _, ga1, sh2, sc2, ga2 = lat
    x_new, h2, aff = merge_and_route(y_a, y_b, proj, off_gate, ctx.shape[1], x, w_branch_a[0], w_branch_b[0],
                                     w_out[0], ga1, g_norm2[0], sh2, sc2, w_router[0])
    return expert_choice_moe(x_new, h2, aff, ga2, w_exp_gate[0], w_exp_up[0], w_exp_down[0])
```

```python
import functools
import math

import jax
import jax.numpy as jnp
from jax import lax
from jax.experimental import pallas as pl
from jax.experimental.pallas import tpu as pltpu

GRID_W = 64
N_HEADS_A = 8
HEAD_DIM_A = 64
N_HEADS_B = 8
HEAD_DIM_B = 128
WIN_H = 8
WIN_W = 16
CAPACITY_FACTOR = 2
ROPE_BASE = 10000.0
EPS = 1e-6
NEG_INF = -1e30

LANES = 128
BF16_SUBLANES = 16
VMEM_LIMIT_BYTES = 56 * 1024 * 1024

F32 = jnp.float32
BF16 = jnp.bfloat16


def _params(*sem):
    return pltpu.CompilerParams(dimension_semantics=sem, vmem_limit_bytes=VMEM_LIMIT_BYTES)


def _pick_tile(n, cap, mult):
    best = None
    for t in range(mult, min(n, cap) + 1, mult):
        if n % t == 0:
            best = t
    assert best is not None, (n, cap, mult)
    return best


def _mod_kernel(cc_ref, w_ref, b_ref, o_ref):
    cc = cc_ref[...]
    act = cc * jax.nn.sigmoid(cc)
    o_ref[...] = jnp.dot(act, w_ref[...], preferred_element_type=F32,
                         precision=lax.Precision.HIGHEST) + b_ref[...]


def adaln_mod(cc, w_mod, b_mod):
    r, d = cc.shape
    n = w_mod.shape[1]
    tn = _pick_tile(n, 1024, LANES)
    return pl.pallas_call(
        _mod_kernel,
        grid=(n // tn,),
        in_specs=[pl.BlockSpec((r, d), lambda j: (0, 0)),
                  pl.BlockSpec((d, tn), lambda j: (0, j)),
                  pl.BlockSpec((1, tn), lambda j: (0, j))],
        out_specs=pl.BlockSpec((r, tn), lambda j: (0, j)),
        out_shape=jax.ShapeDtypeStruct((r, n), F32),
        compiler_params=_params("parallel"),
        name="adaln_mod",
    )(cc, w_mod, b_mod.reshape(1, n))


def _norm_mod_kernel(ctx_ref, x_ref, g_ref, sh_ref, sc_ref, csh_ref, csc_ref, o_ref):
    def emit(src_ref, shift, scale):
        xf = src_ref[0]
        ms = jnp.mean(xf * xf, axis=-1, keepdims=True)
        y = xf * lax.rsqrt(ms + EPS) * g_ref[...]
        o_ref[0] = (y * (1.0 + scale) + shift).astype(o_ref.dtype)

    t = pl.program_id(1)

    @pl.when(t == 0)
    def _():
        emit(ctx_ref, csh_ref[...], csc_ref[...])

    @pl.when(t > 0)
    def _():
        emit(x_ref, sh_ref[0], sc_ref[0])


def norm_modulate_tokens(ctx, x, g, sh, sc, csh, csc):
    b, s, d = x.shape
    l = ctx.shape[1]
    assert s % l == 0
    nt = (l + s) // l
    vec = lambda a: a.reshape(1, d)
    return pl.pallas_call(
        _norm_mod_kernel,
        grid=(b, nt),
        in_specs=[pl.BlockSpec((1, l, d), lambda i, t: (i, 0, 0)),
                  pl.BlockSpec((1, l, d), lambda i, t: (i, jnp.maximum(t - 1, 0), 0)),
                  pl.BlockSpec((1, d), lambda i, t: (0, 0)),
                  pl.BlockSpec((1, 1, d), lambda i, t: (i, 0, 0)),
                  pl.BlockSpec((1, 1, d), lambda i, t: (i, 0, 0)),
                  pl.BlockSpec((1, d), lambda i, t: (0, 0)),
                  pl.BlockSpec((1, d), lambda i, t: (0, 0))],
        out_specs=pl.BlockSpec((1, l, d), lambda i, t: (i, t, 0)),
        out_shape=jax.ShapeDtypeStruct((b, l + s, d), BF16),
        compiler_params=_params("parallel", "arbitrary"),
        name="norm_modulate",
    )(ctx, x, vec(g), sh.reshape(b, 1, d), sc.reshape(b, 1, d), vec(csh), vec(csc))


def _matmul_kernel(x_ref, w_ref, o_ref):
    o_ref[...] = jnp.dot(x_ref[...], w_ref[...], preferred_element_type=F32).astype(o_ref.dtype)


def matmul_bf16(x, w):
    m, k = x.shape
    n = w.shape[1]
    tm = _pick_tile(m, 1024, 8 * BF16_SUBLANES)
    tn = _pick_tile(n, 1024, LANES)
    return pl.pallas_call(
        _matmul_kernel,
        grid=(m // tm, n // tn),
        in_specs=[pl.BlockSpec((tm, k), lambda i, j: (i, 0)),
                  pl.BlockSpec((k, tn), lambda i, j: (0, j))],
        out_specs=pl.BlockSpec((tm, tn), lambda i, j: (i, j)),
        out_shape=jax.ShapeDtypeStruct((m, n), BF16),
        compiler_params=_params("parallel", "arbitrary"),
        name="in_proj",
    )(x, w)


def _dot_f32_by_01(x, mat_ref):
    hi = x.astype(BF16)
    lo = (x - hi.astype(F32)).astype(BF16)
    return (jnp.dot(hi, mat_ref[...], preferred_element_type=F32)
            + jnp.dot(lo, mat_ref[...], preferred_element_type=F32))


def _prep_kernel(xqa, xqb, xka, xva, xkb, xvb, gqa, gka, gqb, gkb, avg_pair, cos_ref, sin_ref, rot_ref,
                 oqa, oqb, oka, ova, okb, ovb, *, scale_a, scale_b):
    def heads_of(x_ref):
        return [(h, x_ref[0, :, h * LANES:(h + 1) * LANES]) for h in range(x_ref.shape[2] // LANES)]

    def diff_head(x, gain_ref, scale):
        x = x.astype(F32)
        x = x * lax.rsqrt(_dot_f32_by_01(x * x, avg_pair) + EPS) * gain_ref[...]
        x = x * cos_ref[...] + _dot_f32_by_01(x, rot_ref) * sin_ref[...]
        return x * scale if scale != 1.0 else x

    def full_head(x, gain_ref, scale):
        x = x.astype(F32)
        x = x * lax.rsqrt(jnp.mean(x * x, axis=-1, keepdims=True) + EPS) * gain_ref[...]
        return x * scale if scale != 1.0 else x

    def with_ones(o_ref, h, x):
        o_ref[0, h, :, 0:LANES] = x
        o_ref[0, h, :, LANES:] = jnp.ones((x.shape[0], o_ref.shape[3] - LANES), o_ref.dtype)

    for h, x in heads_of(xka):
        oka[0, h] = diff_head(x, gka, 1.0).astype(oka.dtype)
    for h, x in heads_of(xkb):
        okb[0, h] = full_head(x, gkb, 1.0).astype(okb.dtype)
    for h, x in heads_of(xva):
        with_ones(ova, h, x)
    for h, x in heads_of(xvb):
        with_ones(ovb, h, x)

    @pl.when(pl.program_id(1) > 0)
    def _():
        for h, x in heads_of(xqa):
            oqa[0, h] = diff_head(x, gqa, scale_a).astype(oqa.dtype)
        for h, x in heads_of(xqb):
            oqb[0, h] = full_head(x, gqb, scale_b).astype(oqb.dtype)


def prep_heads(proj3, offs, n_ctx, gains, rope_tabs, scale_a, scale_b):
    b, t, _ = proj3.shape
    off_qa, off_qb, off_ka, off_va, off_kb, off_vb = offs
    wa, wb = N_HEADS_A * LANES, N_HEADS_B * LANES
    tt = n_ctx
    assert t % tt == 0 and all(o % wa == 0 for o in (off_qa, off_ka, off_va)) and \
        all(o % wb == 0 for o in (off_qb, off_kb, off_vb))
    cols = lambda off, w: pl.BlockSpec((1, tt, w), lambda i, ti: (i, ti, off // w))
    vec = pl.BlockSpec((1, LANES), lambda i, ti: (0, 0))
    square = pl.BlockSpec((LANES, LANES), lambda i, ti: (0, 0))
    tab = pl.BlockSpec((tt, LANES), lambda i, ti: (ti, 0))
    lane = jnp.arange(LANES)
    avg_pair = jnp.where(lane[:, None] // HEAD_DIM_A == lane[None, :] // HEAD_DIM_A, 1.0 / HEAD_DIM_A, 0.0)
    quarter = HEAD_DIM_A // 4
    src, dst = lane[:, None], lane[None, :]
    first = dst % (2 * quarter) < quarter
    rot = jnp.where(first & (src == dst + quarter), -1.0, jnp.where(~first & (src == dst - quarter), 1.0, 0.0))
    q_out = lambda nh: pl.BlockSpec((1, nh, tt, LANES), lambda i, ti: (i, 0, jnp.maximum(ti - 1, 0), 0))
    kv_out = lambda nh, w: pl.BlockSpec((1, nh, tt, w), lambda i, ti: (i, 0, ti, 0))
    shape = lambda nh, n, w: jax.ShapeDtypeStruct((b, nh, n, w), BF16)
    g32 = lambda g: g.reshape(1, LANES).astype(F32)
    return pl.pallas_call(
        functools.partial(_prep_kernel, scale_a=scale_a, scale_b=scale_b),
        grid=(b, t // tt),
        in_specs=[cols(off_qa, wa), cols(off_qb, wb), cols(off_ka, wa), cols(off_va, wa), cols(off_kb, wb),
                  cols(off_vb, wb), vec, vec, vec, vec, square, tab, tab, square],
        out_specs=[q_out(N_HEADS_A), q_out(N_HEADS_B), kv_out(N_HEADS_A, LANES), kv_out(N_HEADS_A, 2 * LANES),
                   kv_out(N_HEADS_B, LANES), kv_out(N_HEADS_B, 2 * LANES)],
        out_shape=[shape(N_HEADS_A, t - n_ctx, LANES), shape(N_HEADS_B, t - n_ctx, LANES),
                   shape(N_HEADS_A, t, LANES), shape(N_HEADS_A, t, 2 * LANES),
                   shape(N_HEADS_B, t, LANES), shape(N_HEADS_B, t, 2 * LANES)],
        compiler_params=_params("parallel", "arbitrary"),
        name="prep_heads",
    )(proj3, proj3, proj3, proj3, proj3, proj3, *[g32(g) for g in gains], avg_pair.astype(BF16),
      rope_tabs[0], rope_tabs[1], rot.astype(BF16))


def rope_tables(n_lat, n_ctx):
    t = jnp.arange(n_lat, dtype=jnp.int32)
    row = (t // GRID_W).astype(F32)
    col = (t % GRID_W).astype(F32)
    half = HEAD_DIM_A // 2
    inv_freq = ROPE_BASE ** (-jnp.arange(0, half, 2, dtype=F32) / half)

    def tab(pos):
        ang = pos[:, None] * inv_freq[None, :]
        ang = jnp.concatenate([ang, ang], axis=-1)
        return jnp.cos(ang), jnp.sin(ang)

    cr, sr = tab(row)
    cc, sc = tab(col)
    cos = jnp.tile(jnp.concatenate([cr, cc], axis=-1), (1, LANES // HEAD_DIM_A))
    sin = jnp.tile(jnp.concatenate([sr, sc], axis=-1), (1, LANES // HEAD_DIM_A))
    pad = lambda a, v: jnp.concatenate([jnp.full((n_ctx, LANES), v, F32), a], axis=0)
    return pad(cos, 1.0), pad(sin, 0.0)


def _diff_attn_kernel(lamv_ref, q_ref, k_ref, v_ref, sg_ref, o_ref, m_s, a_s, qq_s, sa, sb, *, tk, lam_init):
    tq = q_ref.shape[2]
    q = q_ref[0, 0]
    lo = lax.broadcasted_iota(jnp.int32, q.shape, 1) < HEAD_DIM_A
    zero = jnp.zeros_like(q)
    qq_s[0:tq, :] = jnp.where(lo, q, zero)
    qq_s[tq:, :] = jnp.where(lo, zero, q)
    m_s[...] = jnp.full(m_s.shape, -jnp.inf, F32)
    a_s[...] = jnp.zeros(a_s.shape, F32)

    t_all = k_ref.shape[2]
    chunks = [(o, min(tk, t_all - o)) for o in range(0, t_all, tk)]

    def produce(c, s_ref):
        off, size = chunks[c]
        k = k_ref[0, 0, off:off + size, :]
        s_ref[:, 0:size] = lax.dot_general(qq_s[...], k, (((1,), (1,)), ((), ())), preferred_element_type=F32)

    def consume(c, s_ref):
        off, size = chunks[c]
        v = v_ref[0, 0, off:off + size, :]
        s = s_ref[:, 0:size]
        m_old = m_s[...]
        m_new = jnp.maximum(m_old, jnp.max(s, axis=-1, keepdims=True))
        alpha = jnp.exp2(m_old - m_new)
        p = jnp.exp2((s - jnp.tile(m_new, (1, size // LANES))).astype(BF16))
        a_s[...] = jnp.tile(alpha, (1, 2)) * a_s[...] + jnp.dot(p, v, preferred_element_type=F32)
        m_s[...] = m_new

    bufs = (sa, sb)
    produce(0, sa)
    for c in range(len(chunks)):
        if c + 1 < len(chunks):
            produce(c + 1, bufs[(c + 1) % 2])
        consume(c, bufs[c % 2])

    lv = lamv_ref[...]
    lam = (jnp.exp(jnp.sum(lv[0:1] * lv[1:2], axis=-1, keepdims=True))
           - jnp.exp(jnp.sum(lv[2:3] * lv[3:4], axis=-1, keepdims=True)) + lam_init)
    y = (a_s[0:tq, 0:LANES] / a_s[0:tq, LANES:] - lam * (a_s[tq:, 0:LANES] / a_s[tq:, LANES:]))
    y = y * lax.rsqrt(jnp.mean(y * y, axis=-1, keepdims=True) + EPS) * sg_ref[...]
    o_ref[0] = (y * (1.0 - lam_init)).astype(o_ref.dtype)


DIFF_ATTN_TQ = 512
DIFF_ATTN_TK = 1536


def diff_attention(lamv, q, k, v1, subln_gain, lam_init):
    b, h, s, _ = q.shape
    t = k.shape[2]
    tq = _pick_tile(s, DIFF_ATTN_TQ, 8 * BF16_SUBLANES)
    tk = min(DIFF_ATTN_TK, t)
    assert t % LANES == 0 and tk % LANES == 0
    stat = lambda w: pltpu.VMEM((2 * tq, w), F32)
    return pl.pallas_call(
        functools.partial(_diff_attn_kernel, tk=tk, lam_init=lam_init),
        grid=(b, h, s // tq),
        in_specs=[pl.BlockSpec((4, HEAD_DIM_A), lambda i, j, qi: (0, 0)),
                  pl.BlockSpec((1, 1, tq, LANES), lambda i, j, qi: (i, j, qi, 0)),
                  pl.BlockSpec((1, 1, t, LANES), lambda i, j, qi: (i, j, 0, 0)),
                  pl.BlockSpec((1, 1, t, 2 * LANES), lambda i, j, qi: (i, j, 0, 0)),
                  pl.BlockSpec((1, LANES), lambda i, j, qi: (0, 0))],
        out_specs=pl.BlockSpec((1, tq, LANES), lambda i, j, qi: (i, qi, j)),
        out_shape=jax.ShapeDtypeStruct((b, s, h * LANES), BF16),
        scratch_shapes=[stat(LANES), stat(2 * LANES), pltpu.VMEM((2 * tq, LANES), BF16), stat(tk), stat(tk)],
        compiler_params=_params("parallel", "parallel", "arbitrary"),
        name="diff_attention",
    )(lamv, q, k, v1, subln_gain.reshape(1, LANES).astype(F32))


NA_QROWS = 8
NA_KROWS = NA_QROWS + WIN_H
ROWS_PER_VREG = LANES // GRID_W


def _na_kernel(q_ref, k_ref, v_ref, bias_ref, o_ref, sa_ref, sb_ref, *, n_ctx, rows):
    nq, nkw = NA_QROWS * GRID_W, NA_KROWS * GRID_W
    masked = bias_ref.shape[2] - 1
    k_ctx = k_ref[0, 0, 0:n_ctx, :]
    v_ctx = v_ref[0, 0, 0:n_ctx, :]
    dn = (((1,), (1,)), ((), ()))

    def block(j, s_ref):
        r0 = j * NA_QROWS
        w0 = jnp.clip(r0 - WIN_H // 2, 0, rows - NA_KROWS)
        q_off = pl.multiple_of(r0 * GRID_W, nq)
        k_off = pl.multiple_of(n_ctx + w0 * GRID_W, GRID_W)
        q = q_ref[0, 0, pl.ds(q_off, nq), :]
        s_ref[:, 0:n_ctx] = lax.dot_general(q, k_ctx, dn, preferred_element_type=F32)
        s_ref[:, n_ctx:] = lax.dot_general(q, k_ref[0, 0, pl.ds(k_off, nkw), :], dn, preferred_element_type=F32)
        for i in range(NA_QROWS):
            r = r0 + i
            below = r - jnp.clip(r - WIN_H // 2, 0, rows - WIN_H)
            for p in range(NA_KROWS // ROWS_PER_VREG):
                a0 = w0 + p * ROWS_PER_VREG - r + (WIN_H - 1)
                touches = (a0 >= WIN_H - ROWS_PER_VREG - below) & (a0 < 2 * WIN_H - 1 - below)
                entry = jnp.where(touches, a0 + ROWS_PER_VREG - 1, masked)
                cols = slice(n_ctx + p * LANES, n_ctx + (p + 1) * LANES)
                s_ref[i * GRID_W:(i + 1) * GRID_W, cols] += bias_ref[0, below, entry]
        s = s_ref[...]
        p = jnp.exp2(s - jnp.max(s, axis=-1, keepdims=True)).astype(BF16)
        acc = (jnp.dot(p[:, 0:n_ctx], v_ctx, preferred_element_type=F32)
               + jnp.dot(p[:, n_ctx:], v_ref[0, 0, pl.ds(k_off, nkw), :], preferred_element_type=F32))
        o_ref[0, pl.ds(q_off, nq), :] = (acc[:, 0:LANES] / acc[:, LANES:]).astype(o_ref.dtype)

    def body(jj, carry):
        block(2 * jj, sa_ref)
        block(2 * jj + 1, sb_ref)
        return carry

    lax.fori_loop(0, rows // (2 * NA_QROWS), body, 0)


def na_bias_table(rpb):
    n_rel = 2 * WIN_H - 1
    col = jnp.arange(GRID_W, dtype=jnp.int32)
    col_start = jnp.clip(col - WIN_W // 2, 0, GRID_W - WIN_W)
    in_win = (col[None, :] >= col_start[:, None]) & (col[None, :] < col_start[:, None] + WIN_W)
    idx_c = jnp.clip(col[None, :] - col[:, None] + WIN_W - 1, 0, 2 * WIN_W - 2)
    hp = lax.Precision.HIGHEST
    pick_c = (idx_c[:, :, None] == jnp.arange(rpb.shape[2])[None, None, :]).astype(F32)
    toeplitz = jnp.einsum("hac,qkc->haqk", rpb.astype(F32), pick_c, precision=hp)
    below = jnp.arange(WIN_H)[:, None, None]
    entry = jnp.arange(n_rel + ROWS_PER_VREG - 1)[None, :, None]
    rel = entry - (ROWS_PER_VREG - 1) + (jnp.arange(LANES) // GRID_W)[None, None, :]
    valid = (rel >= WIN_H - 1 - below) & (rel < n_rel - below)
    pick_r = ((rel[..., None] == jnp.arange(n_rel)) & valid[..., None]).astype(F32)
    tiles = jnp.einsum("vela,haql->hveql", pick_r, jnp.tile(toeplitz, (1, 1, 1, ROWS_PER_VREG)), precision=hp)
    keep = valid[None, :, :, None, :] & jnp.tile(in_win, (1, ROWS_PER_VREG))[None, None, None]
    tiles = jnp.where(keep, tiles * math.log2(math.e), NEG_INF)
    return jnp.concatenate([tiles, jnp.full_like(tiles[:, :, :1], NEG_INF)], axis=2)


def neighbourhood_attention(q, k, v1, bias, n_ctx):
    b, h, s, _ = q.shape
    t = k.shape[2]
    rows = s // GRID_W
    assert rows % (2 * NA_QROWS) == 0 and rows >= NA_KROWS and LANES % GRID_W == 0
    nq = NA_QROWS * GRID_W
    return pl.pallas_call(
        functools.partial(_na_kernel, n_ctx=n_ctx, rows=rows),
        grid=(b, h),
        in_specs=[pl.BlockSpec((1, 1, s, LANES), lambda i, j: (i, j, 0, 0)),
                  pl.BlockSpec((1, 1, t, LANES), lambda i, j: (i, j, 0, 0)),
                  pl.BlockSpec((1, 1, t, 2 * LANES), lambda i, j: (i, j, 0, 0)),
                  pl.BlockSpec((1,) + bias.shape[1:], lambda i, j: (j, 0, 0, 0, 0))],
        out_specs=pl.BlockSpec((1, s, LANES), lambda i, j: (i, 0, j)),
        out_shape=jax.ShapeDtypeStruct((b, s, h * LANES), BF16),
        scratch_shapes=[pltpu.VMEM((nq, n_ctx + NA_KROWS * GRID_W), F32)] * 2,
        compiler_params=_params("parallel", "parallel"),
        name="neighbourhood_attention",
    )(q, k, v1, bias)


def _attention_branches(x, c, ctx, c_ctx, w_mod, b_mod, g_norm1, w_in, q_gain_a, k_gain_a, lamv, subln_gain,
                        q_gain_b, k_gain_b, rpb, lam_init):
    b, s, d = x.shape
    l = ctx.shape[1]
    t = l + s
    wa, wb = N_HEADS_A * LANES, N_HEADS_B * LANES
    off_qa, off_qb, off_gate = 0, wa, wa + wb
    off_ka = off_gate + 2 * d
    off_va, off_kb, off_vb = off_ka + wa, off_ka + 2 * wa, off_ka + 2 * wa + wb

    n_rows = -(-(b + 1) // 8) * 8
    cc = jnp.zeros((n_rows, d), F32).at[:b].set(c).at[b].set(c_ctx)
    mod = adaln_mod(cc, w_mod, b_mod)
    lat = [mod[:b, i * d:(i + 1) * d] for i in range(6)]
    cmod = [mod[b, i * d:(i + 1) * d] for i in range(6)]

    h = norm_modulate_tokens(ctx, x, g_norm1, lat[0], lat[1], cmod[0], cmod[1])
    proj = matmul_bf16(h.reshape(b * t, d), w_in.astype(BF16)).reshape(b, t, w_in.shape[1])

    two = lambda g: jnp.tile(g, LANES // HEAD_DIM_A)
    log2e = math.log2(math.e)
    qa, qb, ka, va, kb, vb = prep_heads(
        proj, (off_qa, off_qb, off_ka, off_va, off_kb, off_vb), l,
        (two(q_gain_a), two(k_gain_a), q_gain_b, k_gain_b), rope_tables(s, l),
        HEAD_DIM_A ** -0.5 * log2e, HEAD_DIM_B ** -0.5 * log2e)

    y_a = diff_attention(lamv, qa, ka, va, subln_gain, lam_init)
    y_b = neighbourhood_attention(qb, kb, vb, na_bias_table(rpb), l)
    return lat, proj, y_a, y_b, off_gate


def _merge_kernel(ya_ref, yb_ref, ga_ref, gb_ref, x_ref, wa_ref, wb_ref, wo_ref, g1_ref, gn_ref, sh_ref, sc_ref,
                  wrh_ref, wrl_ref, xo_ref, h2_ref, aff_ref, *, n_exp):
    ta = jnp.dot(ya_ref[0], wa_ref[...], preferred_element_type=F32)
    tb = jnp.dot(yb_ref[0], wb_ref[...], preferred_element_type=F32)
    u = jax.nn.sigmoid(ga_ref[0].astype(F32)) * ta + jax.nn.sigmoid(gb_ref[0].astype(F32)) * tb
    mix = jnp.dot(u.astype(BF16), wo_ref[...], preferred_element_type=F32)
    xn = x_ref[0] + g1_ref[0] * mix
    xo_ref[0] = xn
    ms = jnp.mean(xn * xn, axis=-1, keepdims=True)
    h2 = (xn * lax.rsqrt(ms + EPS) * gn_ref[...]) * (1.0 + sc_ref[0]) + sh_ref[0]
    hi = h2.astype(BF16)
    h2_ref[0] = hi
    lo = (h2 - hi.astype(F32)).astype(BF16)
    logits = (jnp.dot(hi, wrh_ref[...], preferred_element_type=F32)
              + jnp.dot(lo, wrh_ref[...], preferred_element_type=F32)
              + jnp.dot(hi, wrl_ref[...], preferred_element_type=F32))
    valid = lax.broadcasted_iota(jnp.int32, logits.shape, 1) < n_exp
    logits = jnp.where(valid, logits, -jnp.inf)
    p = jnp.exp(logits - jnp.max(logits, axis=-1, keepdims=True))
    aff_ref[0] = p / jnp.sum(p, axis=-1, keepdims=True)


def merge_and_route(y_a, y_b, proj3, off_gate, n_ctx, x, w_a, w_b, w_o, ga1, g_norm2, sh2, sc2, w_router):
    b, s, d = x.shape
    n_exp = w_router.shape[1]
    tm = _pick_tile(math.gcd(s, n_ctx), 256, 8 * BF16_SUBLANES)
    assert off_gate % d == 0 and n_exp <= LANES
    tb, gb = n_ctx // tm, off_gate // d
    wr = jnp.zeros((d, LANES), F32).at[:, :n_exp].set(w_router)
    wr_hi = wr.astype(BF16)
    wr_lo = (wr - wr_hi.astype(F32)).astype(BF16)
    const = lambda shape: pl.BlockSpec(shape, lambda i, t: (0,) * len(shape), pipeline_mode=pl.Buffered(1))
    tok = lambda w: pl.BlockSpec((1, tm, w), lambda i, t: (i, t, 0))
    per_b = pl.BlockSpec((1, 1, d), lambda i, t: (i, 0, 0))
    return pl.pallas_call(
        functools.partial(_merge_kernel, n_exp=n_exp),
        grid=(b, s // tm),
        in_specs=[tok(y_a.shape[2]), tok(y_b.shape[2]),
                  pl.BlockSpec((1, tm, d), lambda i, t: (i, t + tb, gb)),
                  pl.BlockSpec((1, tm, d), lambda i, t: (i, t + tb, gb + 1)),
                  tok(d), const(w_a.shape), const(w_b.shape), const(w_o.shape),
                  per_b, const((1, d)), per_b, per_b, const((d, LANES)), const((d, LANES))],
        out_specs=[tok(d), tok(d), tok(LANES)],
        out_shape=[jax.ShapeDtypeStruct((b, s, d), F32), jax.ShapeDtypeStruct((b, s, d), BF16),
                   jax.ShapeDtypeStruct((b, s, LANES), F32)],
        compiler_params=_params("parallel", "arbitrary"),
        name="merge_route",
    )(y_a, y_b, proj3, proj3, x, w_a.astype(BF16), w_b.astype(BF16), w_o.astype(BF16),
      ga1.reshape(b, 1, d), g_norm2.reshape(1, d), sh2.reshape(b, 1, d), sc2.reshape(b, 1, d), wr_hi, wr_lo)


def _prefix_incl(mask, tri_ref, ones_ref, blk_ref):
    e, nc, _ = mask.shape
    m2 = mask.reshape(e * nc, LANES).astype(BF16)
    within = jnp.dot(m2, tri_ref[...], preferred_element_type=F32)
    row_tot = jnp.dot(m2, ones_ref[...], preferred_element_type=F32)
    before = jnp.dot(blk_ref[...], row_tot.astype(BF16), preferred_element_type=F32)
    return (within + before).reshape(e, nc, LANES), before.reshape(e, nc, LANES)


def _topk_kernel(aff_ref, tri_ref, ones_ref, blk_ref, pos_ref, off_ref, *, cap):
    a = aff_ref[0]
    bits = pltpu.bitcast(a, jnp.int32)

    def count(mask):
        c = jnp.sum(jnp.where(mask, 1.0, 0.0), axis=2, keepdims=True)
        return jnp.sum(c, axis=1, keepdims=True)

    def step(i, cur):
        cand = cur | jnp.left_shift(jnp.int32(1), 30 - i)
        return jnp.where(count(bits >= cand) >= cap, cand, cur)

    thr = lax.fori_loop(0, 31, step, jnp.zeros((a.shape[0], 1, 1), jnp.int32))
    gt = bits > thr
    eq = bits == thr
    need = cap - count(gt)
    eq_f = jnp.where(eq, 1.0, 0.0)
    eq_incl, _ = _prefix_incl(eq_f, tri_ref, ones_ref, blk_ref)
    sel = gt | (eq & ((eq_incl - eq_f) < need))
    sel_f = jnp.where(sel, 1.0, 0.0)
    sel_incl, before = _prefix_incl(sel_f, tri_ref, ones_ref, blk_ref)
    pos_ref[0] = jnp.where(sel, sel_incl - sel_f, -1.0).astype(jnp.int32)
    off_ref[0] = before.astype(jnp.int32)


def expert_choice_select(aff_e, cap):
    b, e, s = aff_e.shape
    nc = s // LANES
    idx = jnp.arange(LANES)
    tri = (idx[:, None] <= idx[None, :]).astype(BF16)
    ones = jnp.ones((LANES, LANES), BF16)
    r = jnp.arange(e * nc)
    blk = ((r[:, None] // nc == r[None, :] // nc) & (r[None, :] % nc < r[:, None] % nc)).astype(BF16)
    blk4 = pl.BlockSpec((1, e, nc, LANES), lambda i: (i, 0, 0, 0))
    const = lambda n: pl.BlockSpec((n, n), lambda i: (0, 0))
    pos, off = pl.pallas_call(
        functools.partial(_topk_kernel, cap=cap),
        grid=(b,),
        in_specs=[blk4, const(LANES), const(LANES), const(e * nc)],
        out_specs=[blk4, blk4],
        out_shape=[jax.ShapeDtypeStruct((b, e, nc, LANES), jnp.int32)] * 2,
        compiler_params=_params("parallel"),
        name="expert_choice_select",
    )(aff_e.reshape(b, e, nc, LANES), tri, ones, blk)
    return pos.reshape(b, e, 1, s), off[..., 0]


SEL_TILE = 2 * LANES
SEL_WIN = 64


def _gather_kernel(lo_ref, nw_ref, h_ref, pos_ref, o_ref, *, eg, nt, n_exp):
    i, g, t = pl.program_id(0), pl.program_id(1), pl.program_id(2)

    @pl.when(t == 0)
    def _():
        o_ref[...] = jnp.zeros(o_ref.shape, o_ref.dtype)

    h = h_ref[0]
    slot = lax.broadcasted_iota(jnp.int32, (SEL_WIN, h.shape[0]), 0)

    def onehot(k, base):
        return jnp.where(slot == pos_ref[0, k] - base, 1.0, 0.0).astype(BF16)

    def add_rows(k, base, rows):
        cur = o_ref[0, k, pl.ds(base, SEL_WIN), :]
        o_ref[0, k, pl.ds(base, SEL_WIN), :] = (cur.astype(F32) + rows).astype(o_ref.dtype)

    idx = [(i * n_exp + g * eg + k) * nt + t for k in range(eg)]
    lo = [pl.multiple_of(lo_ref[idx[k]], BF16_SUBLANES) for k in range(eg)]
    first = jnp.dot(jnp.concatenate([onehot(k, lo[k]) for k in range(eg)], axis=0), h,
                    preferred_element_type=F32)
    for k in range(eg):
        add_rows(k, lo[k], first[k * SEL_WIN:(k + 1) * SEL_WIN])

        def extra(w, carry, k=k):
            base = pl.multiple_of(lo[k] + w * SEL_WIN, BF16_SUBLANES)
            add_rows(k, base, jnp.dot(onehot(k, base), h, preferred_element_type=F32))
            return carry

        lax.fori_loop(1, nw_ref[idx[k]], extra, 0)


def expert_gather(h2, pos, lo_flat, nw_flat, cap, eg=4):
    b, s, d = h2.shape
    n_exp = pos.shape[1]
    tm = SEL_TILE
    nt = s // tm
    cp = cap + SEL_WIN
    return pl.pallas_call(
        functools.partial(_gather_kernel, eg=eg, nt=nt, n_exp=n_exp),
        grid_spec=pltpu.PrefetchScalarGridSpec(
            num_scalar_prefetch=2,
            grid=(b, n_exp // eg, nt),
            in_specs=[pl.BlockSpec((1, tm, d), lambda i, g, t, lo, nw: (i, t, 0)),
                      pl.BlockSpec((1, eg, 1, tm), lambda i, g, t, lo, nw: (i, g, 0, t))],
            out_specs=pl.BlockSpec((1, eg, cp, d), lambda i, g, t, lo, nw: (i, g, 0, 0))),
        out_shape=jax.ShapeDtypeStruct((b, n_exp, cp, d), BF16),
        compiler_params=_params("parallel", "parallel", "arbitrary"),
        name="expert_gather",
    )(lo_flat, nw_flat, h2, pos)


def _ffn_kernel(x_ref, wg_ref, wu_ref, wd_ref, o_ref, acc_ref, *, cap):
    f = pl.program_id(2)
    @pl.when(f == 0)
    def _():
        acc_ref[...] = jnp.zeros(acc_ref.shape, F32)

    x = x_ref[0, 0]
    a = jnp.dot(x, wg_ref[0].astype(BF16), preferred_element_type=F32)
    u = jnp.dot(x, wu_ref[0].astype(BF16), preferred_element_type=F32)
    acc_ref[...] += jnp.dot((a * jax.nn.sigmoid(a) * u).astype(BF16), wd_ref[0].astype(BF16),
                            preferred_element_type=F32)

    @pl.when(f == pl.num_programs(2) - 1)
    def _():
        o_ref[0, 0, 0:cap, :] = acc_ref[...].astype(o_ref.dtype)
        o_ref[0, 0, cap:, :] = jnp.zeros((o_ref.shape[2] - cap, o_ref.shape[3]), o_ref.dtype)


FFN_TF = 256


def expert_ffn(xe, w_gate, w_up, w_down, cap):
    b, n_exp, cp, d = xe.shape
    ff = w_gate.shape[2]
    tf = _pick_tile(ff, FFN_TF, LANES)
    return pl.pallas_call(
        functools.partial(_ffn_kernel, cap=cap),
        grid=(n_exp, b, ff // tf),
        in_specs=[pl.BlockSpec((1, 1, cap, d), lambda e, i, f: (i, e, 0, 0)),
                  pl.BlockSpec((1, d, tf), lambda e, i, f: (e, 0, f)),
                  pl.BlockSpec((1, d, tf), lambda e, i, f: (e, 0, f)),
                  pl.BlockSpec((1, tf, d), lambda e, i, f: (e, f, 0))],
        out_specs=pl.BlockSpec((1, 1, cp, d), lambda e, i, f: (i, e, 0, 0)),
        out_shape=jax.ShapeDtypeStruct((b, n_exp, cp, d), BF16),
        scratch_shapes=[pltpu.VMEM((cap, d), F32)],
        compiler_params=_params("parallel", "parallel", "arbitrary"),
        name="expert_ffn",
    )(xe, w_gate, w_up, w_down)


def _combine_kernel(lo_ref, nw_ref, y_hbm, pos_ref, g_ref, x_ref, ga_ref, o_ref, ybuf, yov, acc_ref, sem, osem,
                    *, nt, n_exp):
    i, t = pl.program_id(0), pl.program_id(1)
    tm = x_ref.shape[1]
    step = i * nt + t
    cur = step % 2

    def first_windows(s, buf):
        si, st = s // nt, s % nt
        copies = []
        for e in range(n_exp):
            lo = pl.multiple_of(lo_ref[(si * n_exp + e) * nt + st], BF16_SUBLANES)
            copies.append(pltpu.make_async_copy(y_hbm.at[si, e, pl.ds(lo, SEL_WIN), :],
                                                ybuf.at[buf, pl.ds(e * SEL_WIN, SEL_WIN), :], sem.at[buf, e]))
        return copies

    @pl.when(step == 0)
    def _():
        for copy in first_windows(step, cur):
            copy.start()

    @pl.when(step + 1 < pl.num_programs(0) * nt)
    def _():
        for copy in first_windows(step + 1, 1 - cur):
            copy.start()

    slot = lax.broadcasted_iota(jnp.int32, (SEL_WIN, tm), 0)

    def gated_onehot(e, base):
        return jnp.where(slot == pos_ref[0, e] - base, g_ref[0, e], 0.0).astype(BF16)

    dn = (((0,), (0,)), ((), ()))
    sel = jnp.concatenate([gated_onehot(e, lo_ref[(i * n_exp + e) * nt + t]) for e in range(n_exp)], axis=0)
    for copy in first_windows(step, cur):
        copy.wait()
    acc_ref[...] = lax.dot_general(sel, ybuf[cur], dn, preferred_element_type=F32)

    for e in range(n_exp):
        idx = (i * n_exp + e) * nt + t
        lo = lo_ref[idx]

        def extra(w, carry, e=e, lo=lo):
            base = pl.multiple_of(lo + w * SEL_WIN, BF16_SUBLANES)
            copy = pltpu.make_async_copy(y_hbm.at[i, e, pl.ds(base, SEL_WIN), :], yov, osem.at[0])
            copy.start()
            copy.wait()
            acc_ref[...] += lax.dot_general(gated_onehot(e, base), yov[...], dn, preferred_element_type=F32)
            return carry

        lax.fori_loop(1, nw_ref[idx], extra, 0)

    o_ref[0] = x_ref[0] + ga_ref[0] * acc_ref[...]


def expert_combine(y, pos, gates, lo_flat, nw_flat, x_new, ga2):
    b, s, d = x_new.shape
    n_exp = pos.shape[1]
    tm = SEL_TILE
    nt = s // tm
    sel = pl.BlockSpec((1, n_exp, 1, tm), lambda i, t, lo, nw: (i, 0, 0, t))
    return pl.pallas_call(
        functools.partial(_combine_kernel, nt=nt, n_exp=n_exp),
        grid_spec=pltpu.PrefetchScalarGridSpec(
            num_scalar_prefetch=2,
            grid=(b, nt),
            in_specs=[pl.BlockSpec(memory_space=pl.ANY), sel, sel,
                      pl.BlockSpec((1, tm, d), lambda i, t, lo, nw: (i, t, 0)),
                      pl.BlockSpec((1, 1, d), lambda i, t, lo, nw: (i, 0, 0))],
            out_specs=pl.BlockSpec((1, tm, d), lambda i, t, lo, nw: (i, t, 0)),
            scratch_shapes=[pltpu.VMEM((2, n_exp * SEL_WIN, d), BF16), pltpu.VMEM((SEL_WIN, d), BF16),
                            pltpu.VMEM((tm, d), F32), pltpu.SemaphoreType.DMA((2, n_exp)),
                            pltpu.SemaphoreType.DMA((1,))]),
        out_shape=jax.ShapeDtypeStruct((b, s, d), F32),
        compiler_params=_params("arbitrary", "arbitrary"),
        name="expert_combine",
    )(lo_flat, nw_flat, y, pos, gates, x_new, ga2.reshape(b, 1, d))


def expert_choice_moe(x_new, h2, aff, ga2, w_gate, w_up, w_down):
    b, s, d = x_new.shape
    n_exp = w_gate.shape[0]
    cap = CAPACITY_FACTOR * s // n_exp
    aff_e = aff[:, :, :n_exp].transpose(0, 2, 1)
    pos, off = expert_choice_select(aff_e, cap)
    start = off[:, :, ::SEL_TILE // LANES]
    count = jnp.diff(start, axis=-1, append=jnp.full((b, n_exp, 1), cap, start.dtype))
    lo = (start // BF16_SUBLANES) * BF16_SUBLANES
    nw = jnp.where(count > 0, (start - lo + count + SEL_WIN - 1) // SEL_WIN, 0)
    lo_flat, nw_flat = lo.reshape(-1), nw.reshape(-1)
    xe = expert_gather(h2, pos, lo_flat, nw_flat, cap)
    y = expert_ffn(xe, w_gate, w_up, w_down, cap)
    return expert_combine(y, pos, aff_e.reshape(b, n_exp, 1, s), lo_flat, nw_flat, x_new, ga2)


def kernel(x, c, ctx, c_ctx, w_mod, b_mod, g_norm1, g_norm2, w_in, q_gain_a, k_gain_a, lam_q1, lam_k1, lam_q2,
           lam_k2, subln_gain, q_gain_b, k_gain_b, rel_pos_bias, w_branch_a, w_branch_b, w_out, w_router,
           w_exp_gate, w_exp_up, w_exp_down):
    assert w_mod.shape[0] == 1, "single-layer block"
    lam_init = 0.8 - 0.6 * math.exp(-0.3 * 0)
    lamv = jnp.stack([lam_q1[0], lam_k1[0], lam_q2[0], lam_k2[0]]).astype(F32)
    lat, proj, y_a, y_b, off_gate = _attention_branches(
        x, c, ctx, c_ctx, w_mod[0], b_mod[0], g_norm1[0], w_in[0], q_gain_a[0], k_gain_a[0], lamv, subln_gain[0],
        q_gain_b[0], k_gain_b[0], rel_pos_bias[0], lam_init)
    _, _, ga1, sh2, sc2, ga2 = lat
    x_new, h2, aff = merge_and_route(y_a, y_b, proj, off_gate, ctx.shape[1], x, w_branch_a[0], w_branch_b[0],
                                     w_out[0], ga1, g_norm2[0], sh2, sc2, w_router[0])
    return expert_choice_moe(x_new, h2, aff, ga2, w_exp_gate[0], w_exp_up[0], w_exp_down[0])
```

```python
import functools
import math

import jax
import jax.numpy as jnp
from jax import lax
from jax.experimental import pallas as pl
from jax.experimental.pallas import tpu as pltpu

GRID_W = 64
N_HEADS_A = 8
HEAD_DIM_A = 64
N_HEADS_B = 8
HEAD_DIM_B = 128
WIN_H = 8
WIN_W = 16
CAPACITY_FACTOR = 2
ROPE_BASE = 10000.0
EPS = 1e-6
NEG_INF = -1e30

LANES = 128
BF16_SUBLANES = 16
VMEM_LIMIT_BYTES = 56 * 1024 * 1024

F32 = jnp.float32
BF16 = jnp.bfloat16


def _params(*sem):
    return pltpu.CompilerParams(dimension_semantics=sem, vmem_limit_bytes=VMEM_LIMIT_BYTES)


def _pick_tile(n, cap, mult):
    best = None
    for t in range(mult, min(n, cap) + 1, mult):
        if n % t == 0:
            best = t
    assert best is not None, (n, cap, mult)
    return best


def _mod_kernel(cc_ref, w_ref, b_ref, o_ref):
    cc = cc_ref[...]
    act = cc * jax.nn.sigmoid(cc)
    o_ref[...] = jnp.dot(act, w_ref[...], preferred_element_type=F32,
                         precision=lax.Precision.HIGHEST) + b_ref[...]


def adaln_mod(cc, w_mod, b_mod):
    r, d = cc.shape
    n = w_mod.shape[1]
    tn = _pick_tile(n, 1024, LANES)
    return pl.pallas_call(
        _mod_kernel,
        grid=(n // tn,),
        in_specs=[pl.BlockSpec((r, d), lambda j: (0, 0)),
                  pl.BlockSpec((d, tn), lambda j: (0, j)),
                  pl.BlockSpec((1, tn), lambda j: (0, j))],
        out_specs=pl.BlockSpec((r, tn), lambda j: (0, j)),
        out_shape=jax.ShapeDtypeStruct((r, n), F32),
        compiler_params=_params("parallel"),
        name="adaln_mod",
    )(cc, w_mod, b_mod.reshape(1, n))


def _norm_mod_kernel(ctx_ref, x_ref, g_ref, sh_ref, sc_ref, csh_ref, csc_ref, o_ref):
    def emit(src_ref, shift, scale):
        xf = src_ref[0]
        ms = jnp.mean(xf * xf, axis=-1, keepdims=True)
        y = xf * lax.rsqrt(ms + EPS) * g_ref[...]
        o_ref[0] = (y * (1.0 + scale) + shift).astype(o_ref.dtype)

    t = pl.program_id(1)

    @pl.when(t == 0)
    def _():
        emit(ctx_ref, csh_ref[...], csc_ref[...])

    @pl.when(t > 0)
    def _():
        emit(x_ref, sh_ref[0], sc_ref[0])


def norm_modulate_tokens(ctx, x, g, sh, sc, csh, csc):
    b, s, d = x.shape
    l = ctx.shape[1]
    assert s % l == 0
    nt = (l + s) // l
    vec = lambda a: a.reshape(1, d)
    return pl.pallas_call(
        _norm_mod_kernel,
        grid=(b, nt),
        in_specs=[pl.BlockSpec((1, l, d), lambda i, t: (i, 0, 0)),
                  pl.BlockSpec((1, l, d), lambda i, t: (i, jnp.maximum(t - 1, 0), 0)),
                  pl.BlockSpec((1, d), lambda i, t: (0, 0)),
                  pl.BlockSpec((1, 1, d), lambda i, t: (i, 0, 0)),
                  pl.BlockSpec((1, 1, d), lambda i, t: (i, 0, 0)),
                  pl.BlockSpec((1, d), lambda i, t: (0, 0)),
                  pl.BlockSpec((1, d), lambda i, t: (0, 0))],
        out_specs=pl.BlockSpec((1, l, d), lambda i, t: (i, t, 0)),
        out_shape=jax.ShapeDtypeStruct((b, l + s, d), BF16),
        compiler_params=_params("parallel", "arbitrary"),
        name="norm_modulate",
    )(ctx, x, vec(g), sh.reshape(b, 1, d), sc.reshape(b, 1, d), vec(csh), vec(csc))


def _matmul_kernel(x_ref, w_ref, o_ref):
    o_ref[...] = jnp.dot(x_ref[...], w_ref[...], preferred_element_type=F32).astype(o_ref.dtype)


def matmul_bf16(x, w):
    m, k = x.shape
    n = w.shape[1]
    tm = _pick_tile(m, 1024, 8 * BF16_SUBLANES)
    tn = _pick_tile(n, 1024, LANES)
    return pl.pallas_call(
        _matmul_kernel,
        grid=(m // tm, n // tn),
        in_specs=[pl.BlockSpec((tm, k), lambda i, j: (i, 0)),
                  pl.BlockSpec((k, tn), lambda i, j: (0, j))],
        out_specs=pl.BlockSpec((tm, tn), lambda i, j: (i, j)),
        out_shape=jax.ShapeDtypeStruct((m, n), BF16),
        compiler_params=_params("parallel", "arbitrary"),
        name="in_proj",
    )(x, w)


def _dot_f32_by_01(x, mat_ref):
    hi = x.astype(BF16)
    lo = (x - hi.astype(F32)).astype(BF16)
    return (jnp.dot(hi, mat_ref[...], preferred_element_type=F32)
            + jnp.dot(lo, mat_ref[...], preferred_element_type=F32))


def _prep_kernel(xqa, xqb, xka, xva, xkb, xvb, gqa, gka, gqb, gkb, avg_pair, cos_ref, sin_ref, rot_ref,
                 oqa, oqb, oka, ova, okb, ovb, *, scale_a, scale_b):
    def heads_of(x_ref):
        return [(h, x_ref[0, :, h * LANES:(h + 1) * LANES]) for h in range(x_ref.shape[2] // LANES)]

    def diff_head(x, gain_ref, scale):
        x = x.astype(F32)
        x = x * lax.rsqrt(_dot_f32_by_01(x * x, avg_pair) + EPS) * gain_ref[...]
        x = x * cos_ref[...] + _dot_f32_by_01(x, rot_ref) * sin_ref[...]
        return x * scale if scale != 1.0 else x

    def full_head(x, gain_ref, scale):
        x = x.astype(F32)
        x = x * lax.rsqrt(jnp.mean(x * x, axis=-1, keepdims=True) + EPS) * gain_ref[...]
        return x * scale if scale != 1.0 else x

    def with_ones(o_ref, h, x):
        o_ref[0, h, :, 0:LANES] = x
        o_ref[0, h, :, LANES:] = jnp.ones((x.shape[0], o_ref.shape[3] - LANES), o_ref.dtype)

    for h, x in heads_of(xka):
        oka[0, h] = diff_head(x, gka, 1.0).astype(oka.dtype)
    for h, x in heads_of(xkb):
        okb[0, h] = full_head(x, gkb, 1.0).astype(okb.dtype)
    for h, x in heads_of(xva):
        with_ones(ova, h, x)
    for h, x in heads_of(xvb):
        with_ones(ovb, h, x)

    @pl.when(pl.program_id(1) > 0)
    def _():
        for h, x in heads_of(xqa):
            oqa[0, h] = diff_head(x, gqa, scale_a).astype(oqa.dtype)
        for h, x in heads_of(xqb):
            oqb[0, h] = full_head(x, gqb, scale_b).astype(oqb.dtype)


def prep_heads(proj3, offs, n_ctx, gains, rope_tabs, scale_a, scale_b):
    b, t, _ = proj3.shape
    off_qa, off_qb, off_ka, off_va, off_kb, off_vb = offs
    wa, wb = N_HEADS_A * LANES, N_HEADS_B * LANES
    tt = n_ctx
    assert t % tt == 0 and all(o % wa == 0 for o in (off_qa, off_ka, off_va)) and \
        all(o % wb == 0 for o in (off_qb, off_kb, off_vb))
    cols = lambda off, w: pl.BlockSpec((1, tt, w), lambda i, ti: (i, ti, off // w))
    vec = pl.BlockSpec((1, LANES), lambda i, ti: (0, 0))
    square = pl.BlockSpec((LANES, LANES), lambda i, ti: (0, 0))
    tab = pl.BlockSpec((tt, LANES), lambda i, ti: (ti, 0))
    lane = jnp.arange(LANES)
    avg_pair = jnp.where(lane[:, None] // HEAD_DIM_A == lane[None, :] // HEAD_DIM_A, 1.0 / HEAD_DIM_A, 0.0)
    quarter = HEAD_DIM_A // 4
    src, dst = lane[:, None], lane[None, :]
    first = dst % (2 * quarter) < quarter
    rot = jnp.where(first & (src == dst + quarter), -1.0, jnp.where(~first & (src == dst - quarter), 1.0, 0.0))
    q_out = lambda nh: pl.BlockSpec((1, nh, tt, LANES), lambda i, ti: (i, 0, jnp.maximum(ti - 1, 0), 0))
    kv_out = lambda nh, w: pl.BlockSpec((1, nh, tt, w), lambda i, ti: (i, 0, ti, 0))
    shape = lambda nh, n, w: jax.ShapeDtypeStruct((b, nh, n, w), BF16)
    g32 = lambda g: g.reshape(1, LANES).astype(F32)
    return pl.pallas_call(
        functools.partial(_prep_kernel, scale_a=scale_a, scale_b=scale_b),
        grid=(b, t // tt),
        in_specs=[cols(off_qa, wa), cols(off_qb, wb), cols(off_ka, wa), cols(off_va, wa), cols(off_kb, wb),
                  cols(off_vb, wb), vec, vec, vec, vec, square, tab, tab, square],
        out_specs=[q_out(N_HEADS_A), q_out(N_HEADS_B), kv_out(N_HEADS_A, LANES), kv_out(N_HEADS_A, 2 * LANES),
                   kv_out(N_HEADS_B, LANES), kv_out(N_HEADS_B, 2 * LANES)],
        out_shape=[shape(N_HEADS_A, t - n_ctx, LANES), shape(N_HEADS_B, t - n_ctx, LANES),
                   shape(N_HEADS_A, t, LANES), shape(N_HEADS_A, t, 2 * LANES),
                   shape(N_HEADS_B, t, LANES), shape(N_HEADS_B, t, 2 * LANES)],
        compiler_params=_params("parallel", "arbitrary"),
        name="prep_heads",
    )(proj3, proj3, proj3, proj3, proj3, proj3, *[g32(g) for g in gains], avg_pair.astype(BF16),
      rope_tabs[0], rope_tabs[1], rot.astype(BF16))


def rope_tables(n_lat, n_ctx):
    t = jnp.arange(n_lat, dtype=jnp.int32)
    row = (t // GRID_W).astype(F32)
    col = (t % GRID_W).astype(F32)
    half = HEAD_DIM_A // 2
    inv_freq = ROPE_BASE ** (-jnp.arange(0, half, 2, dtype=F32) / half)

    def tab(pos):
        ang = pos[:, None] * inv_freq[None, :]
        ang = jnp.concatenate([ang, ang], axis=-1)
        return jnp.cos(ang), jnp.sin(ang)

    cr, sr = tab(row)
    cc, sc = tab(col)
    cos = jnp.tile(jnp.concatenate([cr, cc], axis=-1), (1, LANES // HEAD_DIM_A))
    sin = jnp.tile(jnp.concatenate([sr, sc], axis=-1), (1, LANES // HEAD_DIM_A))
    pad = lambda a, v: jnp.concatenate([jnp.full((n_ctx, LANES), v, F32), a], axis=0)
    return pad(cos, 1.0), pad(sin, 0.0)


def _diff_attn_kernel(lamv_ref, q_ref, k_ref, v_ref, sg_ref, o_ref, m_s, a_s, qq_s, sa, sb, *, tk, lam_init):
    tq = q_ref.shape[2]
    q = q_ref[0, 0]
    lo = lax.broadcasted_iota(jnp.int32, q.shape, 1) < HEAD_DIM_A
    zero = jnp.zeros_like(q)
    qq_s[0:tq, :] = jnp.where(lo, q, zero)
    qq_s[tq:, :] = jnp.where(lo, zero, q)
    m_s[...] = jnp.full(m_s.shape, -jnp.inf, F32)
    a_s[...] = jnp.zeros(a_s.shape, F32)

    t_all = k_ref.shape[2]
    chunks = [(o, min(tk, t_all - o)) for o in range(0, t_all, tk)]

    def produce(c, s_ref):
        off, size = chunks[c]
        k = k_ref[0, 0, off:off + size, :]
        s_ref[:, 0:size] = lax.dot_general(qq_s[...], k, (((1,), (1,)), ((), ())), preferred_element_type=F32)

    def consume(c, s_ref):
        off, size = chunks[c]
        v = v_ref[0, 0, off:off + size, :]
        s = s_ref[:, 0:size]
        m_old = m_s[...]
        m_new = jnp.maximum(m_old, jnp.max(s, axis=-1, keepdims=True))
        alpha = jnp.exp2(m_old - m_new)
        p = jnp.exp2((s - jnp.tile(m_new, (1, size // LANES))).astype(BF16))
        a_s[...] = jnp.tile(alpha, (1, 2)) * a_s[...] + jnp.dot(p, v, preferred_element_type=F32)
        m_s[...] = m_new

    bufs = (sa, sb)
    produce(0, sa)
    for c in range(len(chunks)):
        if c + 1 < len(chunks):
            produce(c + 1, bufs[(c + 1) % 2])
        consume(c, bufs[c % 2])

    lv = lamv_ref[...]
    lam = (jnp.exp(jnp.sum(lv[0:1] * lv[1:2], axis=-1, keepdims=True))
           - jnp.exp(jnp.sum(lv[2:3] * lv[3:4], axis=-1, keepdims=True)) + lam_init)
    y = (a_s[0:tq, 0:LANES] / a_s[0:tq, LANES:] - lam * (a_s[tq:, 0:LANES] / a_s[tq:, LANES:]))
    y = y * lax.rsqrt(jnp.mean(y * y, axis=-1, keepdims=True) + EPS) * sg_ref[...]
    o_ref[0] = (y * (1.0 - lam_init)).astype(o_ref.dtype)


DIFF_ATTN_TQ = 512
DIFF_ATTN_TK = 1536


def diff_attention(lamv, q, k, v1, subln_gain, lam_init):
    b, h, s, _ = q.shape
    t = k.shape[2]
    tq = _pick_tile(s, DIFF_ATTN_TQ, 8 * BF16_SUBLANES)
    tk = min(DIFF_ATTN_TK, t)
    assert t % LANES == 0 and tk % LANES == 0
    stat = lambda w: pltpu.VMEM((2 * tq, w), F32)
    return pl.pallas_call(
        functools.partial(_diff_attn_kernel, tk=tk, lam_init=lam_init),
        grid=(b, h, s // tq),
        in_specs=[pl.BlockSpec((4, HEAD_DIM_A), lambda i, j, qi: (0, 0)),
                  pl.BlockSpec((1, 1, tq, LANES), lambda i, j, qi: (i, j, qi, 0)),
                  pl.BlockSpec((1, 1, t, LANES), lambda i, j, qi: (i, j, 0, 0)),
                  pl.BlockSpec((1, 1, t, 2 * LANES), lambda i, j, qi: (i, j, 0, 0)),
                  pl.BlockSpec((1, LANES), lambda i, j, qi: (0, 0))],
        out_specs=pl.BlockSpec((1, tq, LANES), lambda i, j, qi: (i, qi, j)),
        out_shape=jax.ShapeDtypeStruct((b, s, h * LANES), BF16),
        scratch_shapes=[stat(LANES), stat(2 * LANES), pltpu.VMEM((2 * tq, LANES), BF16), stat(tk), stat(tk)],
        compiler_params=_params("parallel", "parallel", "arbitrary"),
        name="diff_attention",
    )(lamv, q, k, v1, subln_gain.reshape(1, LANES).astype(F32))


NA_QROWS = 8
NA_BLOCKS_PER_TRIP = 8
NA_KROWS = NA_QROWS + WIN_H
ROWS_PER_VREG = LANES // GRID_W


def _na_kernel(q_ref, k_ref, v_ref, bias_ref, o_ref, *s_refs, n_ctx, rows):
    nq, nkw = NA_QROWS * GRID_W, NA_KROWS * GRID_W
    masked = bias_ref.shape[2] - 1
    k_ctx = k_ref[0, 0, 0:n_ctx, :]
    v_ctx = v_ref[0, 0, 0:n_ctx, :]
    dn = (((1,), (1,)), ((), ()))

    def block(j, s_ref):
        r0 = j * NA_QROWS
        w0 = jnp.clip(r0 - WIN_H // 2, 0, rows - NA_KROWS)
        q_off = pl.multiple_of(r0 * GRID_W, nq)
        k_off = pl.multiple_of(n_ctx + w0 * GRID_W, GRID_W)
        q = q_ref[0, 0, pl.ds(q_off, nq), :]
        s_ref[:, 0:n_ctx] = lax.dot_general(q, k_ctx, dn, preferred_element_type=F32)
        s_ref[:, n_ctx:] = lax.dot_general(q, k_ref[0, 0, pl.ds(k_off, nkw), :], dn, preferred_element_type=F32)
        for i in range(NA_QROWS):
            r = r0 + i
            below = r - jnp.clip(r - WIN_H // 2, 0, rows - WIN_H)
            for p in range(NA_KROWS // ROWS_PER_VREG):
                a0 = w0 + p * ROWS_PER_VREG - r + (WIN_H - 1)
                touches = (a0 >= WIN_H - ROWS_PER_VREG - below) & (a0 < 2 * WIN_H - 1 - below)
                entry = jnp.where(touches, a0 + ROWS_PER_VREG - 1, masked)
                cols = slice(n_ctx + p * LANES, n_ctx + (p + 1) * LANES)
                s_ref[i * GRID_W:(i + 1) * GRID_W, cols] += bias_ref[0, below, entry]
        s = s_ref[...]
        p = jnp.exp2(s - jnp.max(s, axis=-1, keepdims=True)).astype(BF16)
        acc = (jnp.dot(p[:, 0:n_ctx], v_ctx, preferred_element_type=F32)
               + jnp.dot(p[:, n_ctx:], v_ref[0, 0, pl.ds(k_off, nkw), :], preferred_element_type=F32))
        o_ref[0, pl.ds(q_off, nq), :] = (acc[:, 0:LANES] / acc[:, LANES:]).astype(o_ref.dtype)

    def body(jj, carry):
        for u, s_ref in enumerate(s_refs):
            block(len(s_refs) * jj + u, s_ref)
        return carry

    lax.fori_loop(0, rows // (len(s_refs) * NA_QROWS), body, 0)


def na_bias_table(rpb):
    n_rel = 2 * WIN_H - 1
    col = jnp.arange(GRID_W, dtype=jnp.int32)
    col_start = jnp.clip(col - WIN_W // 2, 0, GRID_W - WIN_W)
    in_win = (col[None, :] >= col_start[:, None]) & (col[None, :] < col_start[:, None] + WIN_W)
    idx_c = jnp.clip(col[None, :] - col[:, None] + WIN_W - 1, 0, 2 * WIN_W - 2)
    hp = lax.Precision.HIGHEST
    pick_c = (idx_c[:, :, None] == jnp.arange(rpb.shape[2])[None, None, :]).astype(F32)
    toeplitz = jnp.einsum("hac,qkc->haqk", rpb.astype(F32), pick_c, precision=hp)
    below = jnp.arange(WIN_H)[:, None, None]
    entry = jnp.arange(n_rel + ROWS_PER_VREG - 1)[None, :, None]
    rel = entry - (ROWS_PER_VREG - 1) + (jnp.arange(LANES) // GRID_W)[None, None, :]
    valid = (rel >= WIN_H - 1 - below) & (rel < n_rel - below)
    pick_r = ((rel[..., None] == jnp.arange(n_rel)) & valid[..., None]).astype(F32)
    tiles = jnp.einsum("vela,haql->hveql", pick_r, jnp.tile(toeplitz, (1, 1, 1, ROWS_PER_VREG)), precision=hp)
    keep = valid[None, :, :, None, :] & jnp.tile(in_win, (1, ROWS_PER_VREG))[None, None, None]
    tiles = jnp.where(keep, tiles * math.log2(math.e), NEG_INF)
    return jnp.concatenate([tiles, jnp.full_like(tiles[:, :, :1], NEG_INF)], axis=2)


def neighbourhood_attention(q, k, v1, bias, n_ctx):
    b, h, s, _ = q.shape
    t = k.shape[2]
    rows = s // GRID_W
    assert rows % NA_QROWS == 0 and rows >= NA_KROWS and LANES % GRID_W == 0
    n_blocks = rows // NA_QROWS
    per_trip = max(u for u in range(1, NA_BLOCKS_PER_TRIP + 1) if n_blocks % u == 0)
    nq = NA_QROWS * GRID_W
    return pl.pallas_call(
        functools.partial(_na_kernel, n_ctx=n_ctx, rows=rows),
        grid=(b, h),
        in_specs=[pl.BlockSpec((1, 1, s, LANES), lambda i, j: (i, j, 0, 0)),
                  pl.BlockSpec((1, 1, t, LANES), lambda i, j: (i, j, 0, 0)),
                  pl.BlockSpec((1, 1, t, 2 * LANES), lambda i, j: (i, j, 0, 0)),
                  pl.BlockSpec((1,) + bias.shape[1:], lambda i, j: (j, 0, 0, 0, 0))],
        out_specs=pl.BlockSpec((1, s, LANES), lambda i, j: (i, 0, j)),
        out_shape=jax.ShapeDtypeStruct((b, s, h * LANES), BF16),
        scratch_shapes=[pltpu.VMEM((nq, n_ctx + NA_KROWS * GRID_W), F32)] * per_trip,
        compiler_params=_params("parallel", "parallel"),
        name="neighbourhood_attention",
    )(q, k, v1, bias)


def _attention_branches(x, c, ctx, c_ctx, w_mod, b_mod, g_norm1, w_in, q_gain_a, k_gain_a, lamv, subln_gain,
                        q_gain_b, k_gain_b, rpb, lam_init):
    b, s, d = x.shape
    l = ctx.shape[1]
    t = l + s
    wa, wb = N_HEADS_A * LANES, N_HEADS_B * LANES
    off_qa, off_qb, off_gate = 0, wa, wa + wb
    off_ka = off_gate + 2 * d
    off_va, off_kb, off_vb = off_ka + wa, off_ka + 2 * wa, off_ka + 2 * wa + wb

    n_rows = -(-(b + 1) // 8) * 8
    cc = jnp.zeros((n_rows, d), F32).at[:b].set(c).at[b].set(c_ctx)
    mod = adaln_mod(cc, w_mod, b_mod)
    lat = [mod[:b, i * d:(i + 1) * d] for i in range(6)]
    cmod = [mod[b, i * d:(i + 1) * d] for i in range(6)]

    h = norm_modulate_tokens(ctx, x, g_norm1, lat[0], lat[1], cmod[0], cmod[1])
    proj = matmul_bf16(h.reshape(b * t, d), w_in.astype(BF16)).reshape(b, t, w_in.shape[1])

    two = lambda g: jnp.tile(g, LANES // HEAD_DIM_A)
    log2e = math.log2(math.e)
    qa, qb, ka, va, kb, vb = prep_heads(
        proj, (off_qa, off_qb, off_ka, off_va, off_kb, off_vb), l,
        (two(q_gain_a), two(k_gain_a), q_gain_b, k_gain_b), rope_tables(s, l),
        HEAD_DIM_A ** -0.5 * log2e, HEAD_DIM_B ** -0.5 * log2e)

    y_a = diff_attention(lamv, qa, ka, va, subln_gain, lam_init)
    y_b = neighbourhood_attention(qb, kb, vb, na_bias_table(rpb), l)
    return lat, proj, y_a, y_b, off_gate


def _merge_kernel(ya_ref, yb_ref, ga_ref, gb_ref, x_ref, wa_ref, wb_ref, wo_ref, g1_ref, gn_ref, sh_ref, sc_ref,
                  wrh_ref, wrl_ref, xo_ref, h2_ref, aff_ref, *, n_exp):
    ta = jnp.dot(ya_ref[0], wa_ref[...], preferred_element_type=F32)
    tb = jnp.dot(yb_ref[0], wb_ref[...], preferred_element_type=F32)
    u = jax.nn.sigmoid(ga_ref[0].astype(F32)) * ta + jax.nn.sigmoid(gb_ref[0].astype(F32)) * tb
    mix = jnp.dot(u.astype(BF16), wo_ref[...], preferred_element_type=F32)
    xn = x_ref[0] + g1_ref[0] * mix
    xo_ref[0] = xn
    ms = jnp.mean(xn * xn, axis=-1, keepdims=True)
    h2 = (xn * lax.rsqrt(ms + EPS) * gn_ref[...]) * (1.0 + sc_ref[0]) + sh_ref[0]
    hi = h2.astype(BF16)
    h2_ref[0] = hi
    lo = (h2 - hi.astype(F32)).astype(BF16)
    logits = (jnp.dot(hi, wrh_ref[...], preferred_element_type=F32)
              + jnp.dot(lo, wrh_ref[...], preferred_element_type=F32)
              + jnp.dot(hi, wrl_ref[...], preferred_element_type=F32))
    valid = lax.broadcasted_iota(jnp.int32, logits.shape, 1) < n_exp
    logits = jnp.where(valid, logits, -jnp.inf)
    p = jnp.exp(logits - jnp.max(logits, axis=-1, keepdims=True))
    aff_ref[0] = p / jnp.sum(p, axis=-1, keepdims=True)


def merge_and_route(y_a, y_b, proj3, off_gate, n_ctx, x, w_a, w_b, w_o, ga1, g_norm2, sh2, sc2, w_router):
    b, s, d = x.shape
    n_exp = w_router.shape[1]
    tm = _pick_tile(math.gcd(s, n_ctx), 256, 8 * BF16_SUBLANES)
    assert off_gate % d == 0 and n_exp <= LANES
    tb, gb = n_ctx // tm, off_gate // d
    wr = jnp.zeros((d, LANES), F32).at[:, :n_exp].set(w_router)
    wr_hi = wr.astype(BF16)
    wr_lo = (wr - wr_hi.astype(F32)).astype(BF16)
    const = lambda shape: pl.BlockSpec(shape, lambda i, t: (0,) * len(shape), pipeline_mode=pl.Buffered(1))
    tok = lambda w: pl.BlockSpec((1, tm, w), lambda i, t: (i, t, 0))
    per_b = pl.BlockSpec((1, 1, d), lambda i, t: (i, 0, 0))
    return pl.pallas_call(
        functools.partial(_merge_kernel, n_exp=n_exp),
        grid=(b, s // tm),
        in_specs=[tok(y_a.shape[2]), tok(y_b.shape[2]),
                  pl.BlockSpec((1, tm, d), lambda i, t: (i, t + tb, gb)),
                  pl.BlockSpec((1, tm, d), lambda i, t: (i, t + tb, gb + 1)),
                  tok(d), const(w_a.shape), const(w_b.shape), const(w_o.shape),
                  per_b, const((1, d)), per_b, per_b, const((d, LANES)), const((d, LANES))],
        out_specs=[tok(d), tok(d), tok(LANES)],
        out_shape=[jax.ShapeDtypeStruct((b, s, d), F32), jax.ShapeDtypeStruct((b, s, d), BF16),
                   jax.ShapeDtypeStruct((b, s, LANES), F32)],
        compiler_params=_params("parallel", "arbitrary"),
        name="merge_route",
    )(y_a, y_b, proj3, proj3, x, w_a.astype(BF16), w_b.astype(BF16), w_o.astype(BF16),
      ga1.reshape(b, 1, d), g_norm2.reshape(1, d), sh2.reshape(b, 1, d), sc2.reshape(b, 1, d), wr_hi, wr_lo)


def _prefix_incl(mask, tri_ref, ones_ref, blk_ref):
    e, nc, _ = mask.shape
    m2 = mask.reshape(e * nc, LANES).astype(BF16)
    within = jnp.dot(m2, tri_ref[...], preferred_element_type=F32)
    row_tot = jnp.dot(m2, ones_ref[...], preferred_element_type=F32)
    before = jnp.dot(blk_ref[...], row_tot.astype(BF16), preferred_element_type=F32)
    return (within + before).reshape(e, nc, LANES), before.reshape(e, nc, LANES)


def _topk_kernel(aff_ref, tri_ref, ones_ref, blk_ref, pos_ref, off_ref, *, cap):
    a = aff_ref[0]
    bits = pltpu.bitcast(a, jnp.int32)

    def count(mask):
        c = jnp.sum(jnp.where(mask, 1.0, 0.0), axis=2, keepdims=True)
        return jnp.sum(c, axis=1, keepdims=True)

    def step(i, cur):
        cand = cur | jnp.left_shift(jnp.int32(1), 30 - i)
        return jnp.where(count(bits >= cand) >= cap, cand, cur)

    thr = lax.fori_loop(0, 31, step, jnp.zeros((a.shape[0], 1, 1), jnp.int32))
    gt = bits > thr
    eq = bits == thr
    need = cap - count(gt)
    eq_f = jnp.where(eq, 1.0, 0.0)
    eq_incl, _ = _prefix_incl(eq_f, tri_ref, ones_ref, blk_ref)
    sel = gt | (eq & ((eq_incl - eq_f) < need))
    sel_f = jnp.where(sel, 1.0, 0.0)
    sel_incl, before = _prefix_incl(sel_f, tri_ref, ones_ref, blk_ref)
    pos_ref[0] = jnp.where(sel, sel_incl - sel_f, -1.0).astype(jnp.int32)
    off_ref[0] = before.astype(jnp.int32)


def expert_choice_select(aff_e, cap):
    b, e, s = aff_e.shape
    nc = s // LANES
    idx = jnp.arange(LANES)
    tri = (idx[:, None] <= idx[None, :]).astype(BF16)
    ones = jnp.ones((LANES, LANES), BF16)
    r = jnp.arange(e * nc)
    blk = ((r[:, None] // nc == r[None, :] // nc) & (r[None, :] % nc < r[:, None] % nc)).astype(BF16)
    blk4 = pl.BlockSpec((1, e, nc, LANES), lambda i: (i, 0, 0, 0))
    const = lambda n: pl.BlockSpec((n, n), lambda i: (0, 0))
    pos, off = pl.pallas_call(
        functools.partial(_topk_kernel, cap=cap),
        grid=(b,),
        in_specs=[blk4, const(LANES), const(LANES), const(e * nc)],
        out_specs=[blk4, blk4],
        out_shape=[jax.ShapeDtypeStruct((b, e, nc, LANES), jnp.int32)] * 2,
        compiler_params=_params("parallel"),
        name="expert_choice_select",
    )(aff_e.reshape(b, e, nc, LANES), tri, ones, blk)
    return pos.reshape(b, e, 1, s), off[..., 0]


SEL_TILE = 2 * LANES
SEL_WIN = 64


def _gather_kernel(lo_ref, nw_ref, h_ref, pos_ref, o_ref, *, eg, nt, n_exp):
    i, g, t = pl.program_id(0), pl.program_id(1), pl.program_id(2)

    @pl.when(t == 0)
    def _():
        o_ref[...] = jnp.zeros(o_ref.shape, o_ref.dtype)

    h = h_ref[0]
    slot = lax.broadcasted_iota(jnp.int32, (SEL_WIN, h.shape[0]), 0)

    def onehot(k, base):
        return jnp.where(slot == pos_ref[0, k] - base, 1.0, 0.0).astype(BF16)

    def add_rows(k, base, rows):
        cur = o_ref[0, k, pl.ds(base, SEL_WIN), :]
        o_ref[0, k, pl.ds(base, SEL_WIN), :] = (cur.astype(F32) + rows).astype(o_ref.dtype)

    idx = [(i * n_exp + g * eg + k) * nt + t for k in range(eg)]
    lo = [pl.multiple_of(lo_ref[idx[k]], BF16_SUBLANES) for k in range(eg)]
    first = jnp.dot(jnp.concatenate([onehot(k, lo[k]) for k in range(eg)], axis=0), h,
                    preferred_element_type=F32)
    for k in range(eg):
        add_rows(k, lo[k], first[k * SEL_WIN:(k + 1) * SEL_WIN])

        def extra(w, carry, k=k):
            base = pl.multiple_of(lo[k] + w * SEL_WIN, BF16_SUBLANES)
            add_rows(k, base, jnp.dot(onehot(k, base), h, preferred_element_type=F32))
            return carry

        lax.fori_loop(1, nw_ref[idx[k]], extra, 0)


def expert_gather(h2, pos, lo_flat, nw_flat, cap, eg=4):
    b, s, d = h2.shape
    n_exp = pos.shape[1]
    tm = SEL_TILE
    nt = s // tm
    cp = cap + SEL_WIN
    return pl.pallas_call(
        functools.partial(_gather_kernel, eg=eg, nt=nt, n_exp=n_exp),
        grid_spec=pltpu.PrefetchScalarGridSpec(
            num_scalar_prefetch=2,
            grid=(b, n_exp // eg, nt),
            in_specs=[pl.BlockSpec((1, tm, d), lambda i, g, t, lo, nw: (i, t, 0)),
                      pl.BlockSpec((1, eg, 1, tm), lambda i, g, t, lo, nw: (i, g, 0, t))],
            out_specs=pl.BlockSpec((1, eg, cp, d), lambda i, g, t, lo, nw: (i, g, 0, 0))),
        out_shape=jax.ShapeDtypeStruct((b, n_exp, cp, d), BF16),
        compiler_params=_params("parallel", "parallel", "arbitrary"),
        name="expert_gather",
    )(lo_flat, nw_flat, h2, pos)


def _ffn_kernel(x_ref, wg_ref, wu_ref, wd_ref, o_ref, acc_ref, *, cap):
    f = pl.program_id(2)
    @pl.when(f == 0)
    def _():
        acc_ref[...] = jnp.zeros(acc_ref.shape, F32)

    x = x_ref[0, 0]
    a = jnp.dot(x, wg_ref[0].astype(BF16), preferred_element_type=F32)
    u = jnp.dot(x, wu_ref[0].astype(BF16), preferred_element_type=F32)
    acc_ref[...] += jnp.dot((a * jax.nn.sigmoid(a) * u).astype(BF16), wd_ref[0].astype(BF16),
                            preferred_element_type=F32)

    @pl.when(f == pl.num_programs(2) - 1)
    def _():
        o_ref[0, 0, 0:cap, :] = acc_ref[...].astype(o_ref.dtype)
        o_ref[0, 0, cap:, :] = jnp.zeros((o_ref.shape[2] - cap, o_ref.shape[3]), o_ref.dtype)


FFN_TF = 256


def expert_ffn(xe, w_gate, w_up, w_down, cap):
    b, n_exp, cp, d = xe.shape
    ff = w_gate.shape[2]
    tf = _pick_tile(ff, FFN_TF, LANES)
    return pl.pallas_call(
        functools.partial(_ffn_kernel, cap=cap),
        grid=(n_exp, b, ff // tf),
        in_specs=[pl.BlockSpec((1, 1, cap, d), lambda e, i, f: (i, e, 0, 0)),
                  pl.BlockSpec((1, d, tf), lambda e, i, f: (e, 0, f)),
                  pl.BlockSpec((1, d, tf), lambda e, i, f: (e, 0, f)),
                  pl.BlockSpec((1, tf, d), lambda e, i, f: (e, f, 0))],
        out_specs=pl.BlockSpec((1, 1, cp, d), lambda e, i, f: (i, e, 0, 0)),
        out_shape=jax.ShapeDtypeStruct((b, n_exp, cp, d), BF16),
        scratch_shapes=[pltpu.VMEM((cap, d), F32)],
        compiler_params=_params("parallel", "parallel", "arbitrary"),
        name="expert_ffn",
    )(xe, w_gate, w_up, w_down)


def _combine_kernel(lo_ref, nw_ref, y_hbm, pos_ref, g_ref, x_ref, ga_ref, o_ref, ybuf, yov, acc_ref, sem, osem,
                    *, nt, n_exp):
    i, t = pl.program_id(0), pl.program_id(1)
    tm = x_ref.shape[1]
    step = i * nt + t
    cur = step % 2

    def first_windows(s, buf):
        si, st = s // nt, s % nt
        copies = []
        for e in range(n_exp):
            lo = pl.multiple_of(lo_ref[(si * n_exp + e) * nt + st], BF16_SUBLANES)
            copies.append(pltpu.make_async_copy(y_hbm.at[si, e, pl.ds(lo, SEL_WIN), :],
                                                ybuf.at[buf, pl.ds(e * SEL_WIN, SEL_WIN), :], sem.at[buf, e]))
        return copies

    @pl.when(step == 0)
    def _():
        for copy in first_windows(step, cur):
            copy.start()

    @pl.when(step + 1 < pl.num_programs(0) * nt)
    def _():
        for copy in first_windows(step + 1, 1 - cur):
            copy.start()

    slot = lax.broadcasted_iota(jnp.int32, (SEL_WIN, tm), 0)

    def gated_onehot(e, base):
        return jnp.where(slot == pos_ref[0, e] - base, g_ref[0, e], 0.0).astype(BF16)

    dn = (((0,), (0,)), ((), ()))
    sel = jnp.concatenate([gated_onehot(e, lo_ref[(i * n_exp + e) * nt + t]) for e in range(n_exp)], axis=0)
    for copy in first_windows(step, cur):
        copy.wait()
    acc_ref[...] = lax.dot_general(sel, ybuf[cur], dn, preferred_element_type=F32)

    for e in range(n_exp):
        idx = (i * n_exp + e) * nt + t
        lo = lo_ref[idx]

        def extra(w, carry, e=e, lo=lo):
            base = pl.multiple_of(lo + w * SEL_WIN, BF16_SUBLANES)
            copy = pltpu.make_async_copy(y_hbm.at[i, e, pl.ds(base, SEL_WIN), :], yov, osem.at[0])
            copy.start()
            copy.wait()
            acc_ref[...] += lax.dot_general(gated_onehot(e, base), yov[...], dn, preferred_element_type=F32)
            return carry

        lax.fori_loop(1, nw_ref[idx], extra, 0)

    o_ref[0] = x_ref[0] + ga_ref[0] * acc_ref[...]


def expert_combine(y, pos, gates, lo_flat, nw_flat, x_new, ga2):
    b, s, d = x_new.shape
    n_exp = pos.shape[1]
    tm = SEL_TILE
    nt = s // tm
    sel = pl.BlockSpec((1, n_exp, 1, tm), lambda i, t, lo, nw: (i, 0, 0, t))
    return pl.pallas_call(
        functools.partial(_combine_kernel, nt=nt, n_exp=n_exp),
        grid_spec=pltpu.PrefetchScalarGridSpec(
            num_scalar_prefetch=2,
            grid=(b, nt),
            in_specs=[pl.BlockSpec(memory_space=pl.ANY), sel, sel,
                      pl.BlockSpec((1, tm, d), lambda i, t, lo, nw: (i, t, 0)),
                      pl.BlockSpec((1, 1, d), lambda i, t, lo, nw: (i, 0, 0))],
            out_specs=pl.BlockSpec((1, tm, d), lambda i, t, lo, nw: (i, t, 0)),
            scratch_shapes=[pltpu.VMEM((2, n_exp * SEL_WIN, d), BF16), pltpu.VMEM((SEL_WIN, d), BF16),
                            pltpu.VMEM((tm, d), F32), pltpu.SemaphoreType.DMA((2, n_exp)),
                            pltpu.SemaphoreType.DMA((1,))]),
        out_shape=jax.ShapeDtypeStruct((b, s, d), F32),
        compiler_params=_params("arbitrary", "arbitrary"),
        name="expert_combine",
    )(lo_flat, nw_flat, y, pos, gates, x_new, ga2.reshape(b, 1, d))


def expert_choice_moe(x_new, h2, aff, ga2, w_gate, w_up, w_down):
    b, s, d = x_new.shape
    n_exp = w_gate.shape[0]
    cap = CAPACITY_FACTOR * s // n_exp
    aff_e = aff[:, :, :n_exp].transpose(0, 2, 1)
    pos, off = expert_choice_select(aff_e, cap)
    start = off[:, :, ::SEL_TILE // LANES]
    count = jnp.diff(start, axis=-1, append=jnp.full((b, n_exp, 1), cap, start.dtype))
    lo = (start // BF16_SUBLANES) * BF16_SUBLANES
    nw = jnp.where(count > 0, (start - lo + count + SEL_WIN - 1) // SEL_WIN, 0)
    lo_flat, nw_flat = lo.reshape(-1), nw.reshape(-1)
    xe = expert_gather(h2, pos, lo_flat, nw_flat, cap)
    y = expert_ffn(xe, w_gate, w_up, w_down, cap)
    return expert_combine(y, pos, aff_e.reshape(b, n_exp, 1, s), lo_flat, nw_flat, x_new, ga2)


def kernel(x, c, ctx, c_ctx, w_mod, b_mod, g_norm1, g_norm2, w_in, q_gain_a, k_gain_a, lam_q1, lam_k1, lam_q2,
           lam_k2, subln_gain, q_gain_b, k_gain_b, rel_pos_bias, w_branch_a, w_branch_b, w_out, w_router,
           w_exp_gate, w_exp_up, w_exp_down):
    assert w_mod.shape[0] == 1, "single-layer block"
    lam_init = 0.8 - 0.6 * math.exp(-0.3 * 0)
    lamv = jnp.stack([lam_q1[0], lam_k1[0], lam_q2[0], lam_k2[0]]).astype(F32)
    lat, proj, y_a, y_b, off_gate = _attention_branches(
        x, c, ctx, c_ctx, w_mod[0], b_mod[0], g_norm1[0], w_in[0], q_gain_a[0], k_gain_a[0], lamv, subln_gain[0],
        q_gain_b[0], k_gain_b[0], rel_pos_bias[0], lam_init)
    _, _, ga1, sh2, sc2, ga2 = lat
    x_new, h2, aff = merge_and_route(y_a, y_b, proj, off_gate, ctx.shape[1], x, w_branch_a[0], w_branch_b[0],
                                     w_out[0], ga1, g_norm2[0], sh2, sc2, w_router[0])
    return expert_choice_moe(x_new, h2, aff, ga2, w_exp_gate[0], w_exp_up[0], w_exp_down[0])
```
